```python
import math
import jax, jax.numpy as jnp
from jax import lax
import numpy as np

D_MODEL = 1024
BATCH = 1
SEQ = 16384
DEPTH = 2

CTX_LEN = 256
GRID_W = 64
MIX = 2 * D_MODEL
CHUNK = 64
EPS = 1e-6

SSD_INNER = MIX // 2
SSD_HEADDIM = 64
SSD_HEADS = SSD_INNER // SSD_HEADDIM
SSD_GROUPS = 2
SSD_STATE = 128
SSD_CONV = 5
SSD_CONV_DIM = SSD_INNER + 2 * SSD_GROUPS * SSD_STATE

GLA_WIDTH = MIX // 4
GLA_HEADS = 4
GLA_KEY = GLA_WIDTH // 2
GLA_HEAD_K = GLA_KEY // GLA_HEADS
GLA_HEAD_V = GLA_WIDTH // GLA_HEADS
GLA_RANK = 16
GLA_NORMALIZER = 16.0

HG_WIDTH = MIX // 4
HG_HEADS = 4
HG_EXPAND = 128
HG_KEY = HG_HEADS * HG_EXPAND
HG_HEAD_V = HG_WIDTH // HG_HEADS

FF = 4 * D_MODEL
N_MOD = 6 * D_MODEL

IN_SIZES = (SSD_INNER, SSD_CONV_DIM, 2 * SSD_HEADS,
            GLA_KEY, GLA_KEY, GLA_WIDTH, GLA_WIDTH, 2 * GLA_RANK,
            HG_KEY, 2 * HG_KEY, HG_WIDTH, HG_WIDTH)
N_IN = sum(IN_SIZES)

kernel_name = "hybrid_ssd_gla_hgrn2_prefix_dit"


def split_points():
    return np.cumsum(IN_SIZES)[:-1].tolist()


def rmsnorm(x, g):
    xf = x.astype(jnp.float32)
    y = xf * lax.rsqrt(jnp.mean(xf * xf, axis=-1, keepdims=True) + EPS)
    return (y * g.astype(jnp.float32)).astype(x.dtype)


def modulate(h, shift, scale):
    return h * (1 + scale) + shift


def centred_dwconv(u, w, bias, rows):
    b, l, ch = u.shape
    if rows is not None:
        u = u.reshape(b * rows, GRID_W, ch)
    pad = w.shape[0] // 2
    y = lax.conv_general_dilated(u, w[:, None, :].astype(u.dtype), (1,), [(pad, pad)],
                                 dimension_numbers=("NWC", "WIO", "NWC"), feature_group_count=ch)
    return y.reshape(b, l, ch) + bias


def segsum_exp(a_cs):
    n = a_cs.shape[-1]
    mask = jnp.tril(jnp.ones((n, n), bool))
    return jnp.exp(jnp.where(mask, a_cs[..., :, None] - a_cs[..., None, :], -jnp.inf))


def ssd_scan(xdt, a, bm, cm, s0):
    b, l, h, p = xdt.shape
    g, n = bm.shape[-2:]
    j = h // g
    c = l // CHUNK
    X = xdt.astype(jnp.float32).reshape(b, c, CHUNK, g, j, p)
    A = a.astype(jnp.float32).reshape(b, c, CHUNK, g, j).transpose(0, 3, 4, 1, 2)
    Bc = bm.astype(jnp.float32).reshape(b, c, CHUNK, g, n)
    Cc = cm.astype(jnp.float32).reshape(b, c, CHUNK, g, n)
    a_cs = jnp.cumsum(A, axis=-1)
    scores = jnp.einsum("bclgn,bcsgn->bgcls", Cc, Bc)
    y_diag = jnp.einsum("bgcls,bgjcls,bcsgjp->bclgjp", scores, segsum_exp(a_cs), X)
    decay_states = jnp.exp(a_cs[..., -1:] - a_cs)
    states = jnp.einsum("bcsgn,bgjcs,bcsgjp->cbgjpn", Bc, decay_states, X)
    chunk_decay = jnp.exp(a_cs[..., -1]).transpose(3, 0, 1, 2)

    def step(S, inp):
        dec, st = inp
        return dec[..., None, None] * S + st, S

    s_final, s_in = lax.scan(step, s0.astype(jnp.float32).reshape(b, g, j, p, n), (chunk_decay, states))
    y_off = jnp.einsum("bclgn,bgjcl,cbgjpn->bclgjp", Cc, jnp.exp(a_cs), s_in)
    return (y_diag + y_off).reshape(b, l, h, p), s_final.reshape(b, h, p, n)


def gla_scan(q, k, v, log_a, s0):
    b, l, h, dk = q.shape
    dv = v.shape[-1]
    c = l // CHUNK

    def to_chunks(t):
        return t.astype(jnp.float32).reshape(b, c, CHUNK, h, t.shape[-1]).transpose(1, 0, 3, 2, 4)

    mask = jnp.tril(jnp.ones((CHUNK, CHUNK), bool))[:, :, None]

    def step(S, inp):
        qc, kc, vc, gc = inp
        bcs = jnp.cumsum(gc, axis=-2)
        inter = jnp.einsum("bhld,bhde->bhle", qc * jnp.exp(bcs), S)
        pair = jnp.exp(jnp.where(mask, bcs[:, :, :, None, :] - bcs[:, :, None, :, :], -jnp.inf))
        att = jnp.einsum("bhld,bhsd,bhlsd->bhls", qc, kc, pair)
        intra = jnp.einsum("bhls,bhse->bhle", att, vc)
        last = bcs[:, :, -1:, :]
        S = jnp.exp(last[:, :, 0, :])[..., None] * S + jnp.einsum("bhsd,bhse->bhde", kc * jnp.exp(last - bcs), vc)
        return S, inter + intra

    s_final, o = lax.scan(step, s0.astype(jnp.float32), (to_chunks(q), to_chunks(k), to_chunks(v), to_chunks(log_a)))
    return o.transpose(1, 0, 3, 2, 4).reshape(b, l, h, dv), s_final


def bidir_prefix_scan(scan_fn, ctx_dirs, lat_dirs, s0):
    flip = lambda t: jnp.flip(t, axis=1)
    yc_f, sc_f = scan_fn(*ctx_dirs[0], s0)
    yl_f, _ = scan_fn(*lat_dirs[0], sc_f)
    yc_b, sc_b = scan_fn(*[flip(t) for t in ctx_dirs[1]], s0)
    yl_b, _ = scan_fn(*[flip(t) for t in lat_dirs[1]], sc_b)
    return yc_f + flip(yc_b), yl_f + flip(yl_b)


def mixer_inputs(h, w_in, conv_w, conv_b, dt_bias, a_log, gla_w_gk2, gla_b_gk, lb, rows):
    f32 = jnp.float32
    b, l, _ = h.shape
    z, xbc, dt, gq, gk, gv, gg, glr, hq, hf, hi, hgate = jnp.split(h @ w_in, split_points(), axis=-1)
    xbc = jax.nn.silu(centred_dwconv(xbc, conv_w, conv_b, rows))
    xs, bm, cm = jnp.split(xbc, [SSD_INNER, SSD_INNER + SSD_GROUPS * SSD_STATE], axis=-1)
    xs = xs.reshape(b, l, SSD_HEADS, SSD_HEADDIM).astype(f32)
    bm = bm.reshape(b, l, SSD_GROUPS, SSD_STATE)
    cm = cm.reshape(b, l, SSD_GROUPS, SSD_STATE)
    dt = jax.nn.softplus(dt.reshape(b, l, 2, SSD_HEADS).astype(f32) + dt_bias.astype(f32))
    a = -jnp.exp(a_log.astype(f32))
    ssd_dirs = tuple((xs * dt[:, :, d, :, None], dt[:, :, d] * a[d], bm, cm) for d in range(2))
    q = gq.reshape(b, l, GLA_HEADS, GLA_HEAD_K).astype(f32) * GLA_HEAD_K ** -0.5
    k = gk.reshape(b, l, GLA_HEADS, GLA_HEAD_K).astype(f32)
    v = gv.reshape(b, l, GLA_HEADS, GLA_HEAD_V).astype(f32)
    gate_logit = jnp.einsum("bldr,drk->bldk", glr.reshape(b, l, 2, GLA_RANK), gla_w_gk2) + gla_b_gk
    log_a = (jax.nn.log_sigmoid(gate_logit.astype(f32)) / GLA_NORMALIZER).reshape(b, l, 2, GLA_HEADS, GLA_HEAD_K)
    gla_dirs = tuple((q, k, v, log_a[:, :, d]) for d in range(2))
    lbh = lb.astype(f32).reshape(HG_HEADS, HG_EXPAND)
    hq = jax.nn.silu(hq.astype(f32)).reshape(b, l, HG_HEADS, HG_EXPAND)
    hf = hf.reshape(b, l, 2, HG_HEADS, HG_EXPAND).astype(f32)
    log_f = jnp.logaddexp(jnp.log(lbh), jnp.log1p(-lbh) + jax.nn.log_sigmoid(hf))
    k_hg = (1.0 - lbh) * jax.nn.sigmoid(-hf)
    hi = hi.reshape(b, l, HG_HEADS, HG_HEAD_V).astype(f32)
    hg_dirs = tuple((hq, k_hg[:, :, d], hi, log_f[:, :, d]) for d in range(2))
    return (ssd_dirs, gla_dirs, hg_dirs), (xs, z, gg, hgate)


def mixer_output(y_ssd, y_gla, y_hg, post, d_skip, ssd_norm_g, gla_norm_g, hg_norm_g, w_out, dtype):
    f32 = jnp.float32
    xs, z, gg, hgate = post
    b, l = z.shape[:2]
    y_s = (y_ssd + d_skip.astype(f32)[:, None] * xs).reshape(b, l, SSD_INNER)
    y_s = rmsnorm(y_s * jax.nn.silu(z.astype(f32)), ssd_norm_g)
    y_g = rmsnorm(y_gla, gla_norm_g).reshape(b, l, GLA_WIDTH) * jax.nn.silu(gg.astype(f32))
    y_h = rmsnorm(y_hg, hg_norm_g).reshape(b, l, HG_WIDTH) * jax.nn.sigmoid(hgate.astype(f32))
    return jnp.concatenate([y_s, y_g, y_h], axis=-1).astype(dtype) @ w_out


def hybrid_mixer(hc, hl, w_in, conv_w, conv_b, dt_bias, a_log, d_skip, ssd_norm_g, gla_w_gk2, gla_b_gk,
                 gla_norm_g, lb, hg_norm_g, w_out, rows, with_ctx):
    b = hl.shape[0]
    ctx_dirs, ctx_post = mixer_inputs(hc, w_in, conv_w, conv_b, dt_bias, a_log, gla_w_gk2, gla_b_gk, lb, None)
    lat_dirs, lat_post = mixer_inputs(hl, w_in, conv_w, conv_b, dt_bias, a_log, gla_w_gk2, gla_b_gk, lb, rows)
    zeros = lambda *s: jnp.zeros((b,) + s, jnp.float32)
    yc_s, yl_s = bidir_prefix_scan(ssd_scan, ctx_dirs[0], lat_dirs[0], zeros(SSD_HEADS, SSD_HEADDIM, SSD_STATE))
    yc_g, yl_g = bidir_prefix_scan(gla_scan, ctx_dirs[1], lat_dirs[1], zeros(GLA_HEADS, GLA_HEAD_K, GLA_HEAD_V))
    yc_h, yl_h = bidir_prefix_scan(gla_scan, ctx_dirs[2], lat_dirs[2], zeros(HG_HEADS, HG_EXPAND, HG_HEAD_V))
    y_lat = mixer_output(yl_s, yl_g, yl_h, lat_post, d_skip, ssd_norm_g, gla_norm_g, hg_norm_g, w_out, hl.dtype)
    y_ctx = None
    if with_ctx:
        y_ctx = mixer_output(yc_s, yc_g, yc_h, ctx_post, d_skip, ssd_norm_g, gla_norm_g, hg_norm_g, w_out, hc.dtype)
    return y_ctx, y_lat


def sq_relu_mlp(h, w1, w2):
    return jnp.square(jax.nn.relu(h @ w1)) @ w2


def setup_inputs(seed: int = 0) -> dict:
    key = jax.random.key(seed)
    ks = jax.random.split(key, 32)
    f32 = jnp.float32
    nrm = lambda k, shape, scale: jax.random.normal(k, shape, f32) * scale
    L = DEPTH
    dt0 = jnp.exp(jax.random.uniform(ks[10], (L, 2, SSD_HEADS), f32, math.log(1e-3), math.log(1e-1)))
    return {
        "x": nrm(ks[0], (BATCH, SEQ, D_MODEL), 1.0),
        "c": nrm(ks[1], (BATCH, D_MODEL), 1.0),
        "ctx": nrm(ks[2], (BATCH, CTX_LEN, D_MODEL), 1.0),
        "c_ctx": nrm(ks[3], (D_MODEL,), 1.0),
        "norm1_g": 1.0 + nrm(ks[4], (L, D_MODEL), 0.02),
        "norm2_g": 1.0 + nrm(ks[5], (L, D_MODEL), 0.02),
        "w_mod": nrm(ks[6], (L, D_MODEL, N_MOD), 0.3 * D_MODEL ** -0.5),
        "b_mod": nrm(ks[7], (L, N_MOD), 0.01),
        "w_in": nrm(ks[8], (L, D_MODEL, N_IN), D_MODEL ** -0.5),
        "ssd_conv_w": nrm(ks[9], (L, SSD_CONV, SSD_CONV_DIM), SSD_CONV ** -0.5),
        "ssd_conv_b": nrm(ks[11], (L, SSD_CONV_DIM), 0.01),
        "ssd_dt_bias": dt0 + jnp.log(-jnp.expm1(-dt0)),
        "ssd_a_log": jnp.log(jax.random.uniform(ks[12], (L, 2, SSD_HEADS), f32, 1.0, 16.0)),
        "ssd_d": 1.0 + nrm(ks[13], (L, SSD_HEADS), 0.02),
        "ssd_norm_g": 1.0 + nrm(ks[14], (L, SSD_INNER), 0.02),
        "gla_w_gk2": nrm(ks[15], (L, 2, GLA_RANK, GLA_KEY), GLA_RANK ** -0.5),
        "gla_b_gk": nrm(ks[16], (L, 2, GLA_KEY), 0.01),
        "gla_norm_g": 1.0 + nrm(ks[17], (L, GLA_HEAD_V), 0.02),
        "hg_lb_logits": nrm(ks[18], (L, HG_KEY), 0.5),
        "hg_norm_g": 1.0 + nrm(ks[19], (L, HG_HEAD_V), 0.02),
        "w_out": nrm(ks[20], (L, MIX, D_MODEL), MIX ** -0.5),
        "w_mlp1": nrm(ks[21], (L, D_MODEL, FF), D_MODEL ** -0.5),
        "w_mlp2": nrm(ks[22], (L, FF, D_MODEL), FF ** -0.5),
        "final_norm_g": 1.0 + nrm(ks[23], (D_MODEL,), 0.02),
    }


def reference(x, c, ctx, c_ctx, norm1_g, norm2_g, w_mod, b_mod, w_in, ssd_conv_w, ssd_conv_b, ssd_dt_bias,
              ssd_a_log, ssd_d, ssd_norm_g, gla_w_gk2, gla_b_gk, gla_norm_g, hg_lb_logits, hg_norm_g, w_out,
              w_mlp1, w_mlp2, final_norm_g):
    seq = x.shape[1]
    rows = seq // GRID_W
    lbs = jnp.cumsum(jax.nn.softmax(hg_lb_logits.astype(jnp.float32), axis=0), axis=0)
    lbs = lbs - lbs[0]
    h_ctx = ctx
    for layer in range(DEPTH):
        last = layer == DEPTH - 1
        mod_lat = jax.nn.silu(c) @ w_mod[layer] + b_mod[layer]
        mod_ctx = jax.nn.silu(c_ctx) @ w_mod[layer] + b_mod[layer]
        sh1, sc1, g1, sh2, sc2, g2 = jnp.split(mod_lat[:, None, :], 6, axis=-1)
        csh1, csc1, cg1, csh2, csc2, cg2 = jnp.split(mod_ctx, 6, axis=-1)
        hl = modulate(rmsnorm(x, norm1_g[layer]), sh1, sc1)
        hc = modulate(rmsnorm(h_ctx, norm1_g[layer]), csh1, csc1)
        y_ctx, y_lat = hybrid_mixer(hc, hl, w_in[layer], ssd_conv_w[layer], ssd_conv_b[layer], ssd_dt_bias[layer],
                                    ssd_a_log[layer], ssd_d[layer], ssd_norm_g[layer], gla_w_gk2[layer],
                                    gla_b_gk[layer], gla_norm_g[layer], lbs[layer], hg_norm_g[layer], w_out[layer],
                                    rows, not last)
        x = x + g1 * y_lat
        x = x + g2 * sq_relu_mlp(modulate(rmsnorm(x, norm2_g[layer]), sh2, sc2), w_mlp1[layer], w_mlp2[layer])
        if not last:
            h_ctx = h_ctx + cg1 * y_ctx
            h_ctx = h_ctx + cg2 * sq_relu_mlp(modulate(rmsnorm(h_ctx, norm2_g[layer]), csh2, csc2),
                                              w_mlp1[layer], w_mlp2[layer])
    return rmsnorm(x, final_norm_g)
```

```python
import functools

import numpy as np
import jax
import jax.numpy as jnp
from jax import lax
from jax.experimental import pallas as pl
from jax.experimental.pallas import tpu as pltpu

F32 = jnp.float32
BF16 = jnp.bfloat16

D = 1024
CHUNK = 64
EPS = 1e-6
SSD_INNER = 1024
SSD_HEADS = 16
SSD_HEADDIM = 64
SSD_STATE = 128
SSD_GROUPS = 2
SSD_CONV = 5
SSD_CONV_DIM = 1536
GLA_HEADS = 4
GLA_HEAD_K = 64
GLA_RANK = 16
GLA_NORMALIZER = 16.0
HEADS = 4
HEAD_W = 128
MIXW = HEADS * HEAD_W
FF = 4096
N_LEVELS = 6

VMEM_LIMIT = 56 * 1024 * 1024
TM_PROJ = 256
TB_SCAN = 256

_IN_SIZES = (1024, 1536, 32, 256, 256, 512, 512, 32, 512, 1024, 512, 512)
_IN_OFFS = np.concatenate([[0], np.cumsum(_IN_SIZES)]).tolist()

_P_Z, _P_XBC, _P_GQ, _P_GK, _P_GV, _P_GG, _P_HQ, _P_HF, _P_HI, _P_HGATE, _P_SMALL, _P_END = (
    0, 1024, 2560, 3072, 3584, 4096, 4608, 5120, 6144, 6656, 7168, 7296)


def _chunk_constants():
    n = CHUNK
    p = np.arange(n)[:, None]
    r = np.arange(n)[None, :]
    blocks = [(r <= p)]
    masks = [(p == r)]
    for lev in range(1, N_LEVELS + 1):
        size = 1 << lev
        half = size // 2
        start = (p // size) * size
        mid = start + half - 1
        second = (p - start) >= half
        m = np.where(second, (r > mid) & (r <= p), (r > p) & (r <= mid))
        blocks.append(m)
        same = (p // size) == (r // size)
        masks.append(same & second & ((r - (r // size) * size) < half))
    blocks.append(r > p)
    fwd = np.concatenate([b.astype(np.float32) for b in blocks], axis=0)
    fmask = np.stack([m.astype(np.float32) for m in masks], axis=0)
    bwd = np.concatenate([b.astype(np.float32)[::-1, ::-1] for b in blocks], axis=0)
    bmask = fmask[:, ::-1, ::-1]
    cm = np.stack([fwd, bwd], axis=0)
    mk = np.stack([fmask, bmask], axis=0)
    return cm, np.ascontiguousarray(mk)


def _ssd_constants():
    n = CHUNK
    p = np.arange(n)[:, None]
    r = np.arange(n)[None, :]
    cum_f = (r <= p).astype(np.float32)
    edge_f = (r > p).astype(np.float32)
    cm = np.stack([np.concatenate([cum_f, edge_f], 0),
                   np.concatenate([cum_f[::-1, ::-1], edge_f[::-1, ::-1]], 0)], 0)
    lanes = np.arange(SSD_INNER)[None, :]
    expand = (np.arange(SSD_HEADS)[:, None] == lanes // SSD_HEADDIM).astype(np.float32)
    s_of_lane = lanes % SSD_HEADDIM
    diag = (p == s_of_lane).astype(np.float32)
    causal = np.stack([(p >= s_of_lane), (p <= s_of_lane)], 0).astype(np.float32)
    return cm, expand, diag, causal


_CM_NP, _MASK_NP = _chunk_constants()
_SSD_CM_NP, _EXPAND_NP, _DIAG_NP, _CAUSAL_NP = _ssd_constants()


def _sigmoid(x):
    return 1.0 / (1.0 + jnp.exp(-x))


def _silu(x):
    return x * _sigmoid(x)


def _log_sigmoid(x):
    return jnp.minimum(x, 0.0) - jnp.log1p(jnp.exp(-jnp.abs(x)))


def _split3(x):
    hi = x.astype(BF16)
    r1 = x - hi.astype(F32)
    mid = r1.astype(BF16)
    r2 = r1 - mid.astype(F32)
    lo = r2.astype(BF16)
    return hi, mid, lo


def _exact_left_mul(m01, x):
    hi, mid, lo = _split3(x)
    return (jnp.dot(m01, hi, preferred_element_type=F32)
            + jnp.dot(m01, mid, preferred_element_type=F32)
            + jnp.dot(m01, lo, preferred_element_type=F32))


def _dot_nt(a, b):
    return lax.dot_general(a, b, (((1,), (1,)), ((), ())), preferred_element_type=F32)


def _dot_tn(a, b):
    return lax.dot_general(a, b, (((0,), (0,)), ((), ())), preferred_element_type=F32)


def _rms(x, gain):
    return x * lax.rsqrt(jnp.mean(x * x, axis=-1, keepdims=True) + EPS) * gain


def _mod_kernel(cc_ref, w_ref, b_ref, o_ref):
    s = _silu(cc_ref[...])
    o_ref[0] = jnp.dot(s, w_ref[0], precision=lax.Precision.HIGHEST, preferred_element_type=F32) + b_ref[0]


def _modulation(cc, w_mod, b_mod):
    depth, _, n_mod = w_mod.shape
    bn = 1536
    return pl.pallas_call(
        _mod_kernel,
        grid=(depth, n_mod // bn),
        in_specs=[pl.BlockSpec((8, D), lambda l, j: (0, 0)),
                  pl.BlockSpec((1, D, bn), lambda l, j: (l, 0, j)),
                  pl.BlockSpec((1, 1, bn), lambda l, j: (l, 0, j))],
        out_specs=pl.BlockSpec((1, 8, bn), lambda l, j: (l, 0, j)),
        out_shape=jax.ShapeDtypeStruct((depth, 8, n_mod), F32),
        compiler_params=pltpu.CompilerParams(dimension_semantics=("arbitrary", "arbitrary"),
                                             vmem_limit_bytes=VMEM_LIMIT),
        name="modulation",
    )(cc, w_mod, b_mod.reshape(depth, 1, n_mod))


def _inproj_kernel(x_ref, g_ref, sh_ref, sc_ref, w_ref, cw_ref, cb_ref,
                   z_ref, xs_ref, bc_ref, gqk_ref, gv_ref, gg_ref, hq_ref, hf_ref, hi_ref, hgate_ref,
                   small_ref, *, row_len):
    x = x_ref[...]
    tm = x.shape[0]
    h = _rms(x, g_ref[...]) * (1.0 + sc_ref[...]) + sh_ref[...]
    hb = h.astype(BF16)

    def proj(a, b):
        return jnp.dot(hb, w_ref[:, a:b], preferred_element_type=F32)

    z_ref[...] = proj(_P_Z, _P_XBC)

    u = proj(_P_XBC, _P_GQ)
    pos = lax.broadcasted_iota(jnp.int32, (tm, 128), 0) & (row_len - 1)
    shifts = [k - SSD_CONV // 2 for k in range(SSD_CONV)]
    valid = {s: ((pos + s >= 0) & (pos + s < row_len)) for s in shifts if s != 0}
    for j in range(SSD_CONV_DIM // 128):
        lo = j * 128
        uj = u[:, lo:lo + 128]
        acc = uj * cw_ref[SSD_CONV // 2:SSD_CONV // 2 + 1, lo:lo + 128] + cb_ref[:, lo:lo + 128]
        for k, s in enumerate(shifts):
            if s == 0:
                continue
            us = pltpu.roll(uj, (-s) % tm, axis=0)
            acc = acc + jnp.where(valid[s], us, 0.0) * cw_ref[k:k + 1, lo:lo + 128]
        v = _silu(acc)
        if lo < SSD_INNER:
            xs_ref[:, lo:lo + 128] = v
        else:
            bc_ref[:, lo - SSD_INNER:lo - SSD_INNER + 128] = v.astype(BF16)

    q = proj(_P_GQ, _P_GK)
    gqk_ref[:, 0:MIXW] = q * (GLA_HEAD_K ** -0.5)
    gqk_ref[:, MIXW:2 * MIXW] = proj(_P_GK, _P_GV)
    gv_ref[...] = proj(_P_GV, _P_GG).astype(BF16)
    gg_ref[...] = proj(_P_GG, _P_HQ)
    hq_ref[...] = _silu(proj(_P_HQ, _P_HF))
    hf_ref[...] = proj(_P_HF, _P_HI)
    hi_ref[...] = proj(_P_HI, _P_HGATE).astype(BF16)
    hgate_ref[...] = proj(_P_HGATE, _P_SMALL)
    small_ref[...] = proj(_P_SMALL, _P_END)


def _inproj(x2, gain, shift, scale, w_p, conv_w, conv_b, row_len):
    t = x2.shape[0]
    tm = min(TM_PROJ, t)
    row = lambda w: pl.BlockSpec((1, w), lambda i: (0, 0))
    tile = lambda w: pl.BlockSpec((tm, w), lambda i: (i, 0))
    outs = [(D, F32), (SSD_INNER, F32), (512, BF16), (2 * MIXW, F32), (MIXW, BF16), (MIXW, F32),
            (MIXW, F32), (2 * MIXW, F32), (MIXW, BF16), (MIXW, F32), (128, F32)]
    return pl.pallas_call(
        functools.partial(_inproj_kernel, row_len=row_len),
        grid=(t // tm,),
        in_specs=[tile(D), row(D), row(D), row(D),
                  pl.BlockSpec((D, _P_END), lambda i: (0, 0), pipeline_mode=pl.Buffered(1)),
                  pl.BlockSpec((SSD_CONV, SSD_CONV_DIM), lambda i: (0, 0)),
                  row(SSD_CONV_DIM)],
        out_specs=[tile(w) for w, _ in outs],
        out_shape=[jax.ShapeDtypeStruct((t, w), dt) for w, dt in outs],
        compiler_params=pltpu.CompilerParams(dimension_semantics=("arbitrary",), vmem_limit_bytes=VMEM_LIMIT),
        name="inproj",
    )(x2, gain, shift, scale, w_p, conv_w, conv_b)


def _gla_chunk(q, k, v, g, st_ref, d, cm_ref, mask_ref):
    e = _exact_left_mul(cm_ref[d], g)
    edge_row = (CHUNK - 1) if d == 0 else 0
    outs = []
    for h in range(HEADS):
        sl = slice(h * HEAD_W, (h + 1) * HEAD_W)
        qh, kh, vh = q[:, sl], k[:, sl], v[:, sl]
        eh = e[:, sl]
        att = mask_ref[d, 0] * _dot_nt(qh.astype(BF16), kh.astype(BF16))
        for lev in range(1, N_LEVELS + 1):
            xl = jnp.exp(eh[lev * CHUNK:(lev + 1) * CHUNK])
            att = att + mask_ref[d, lev] * _dot_nt((qh * xl).astype(BF16), (kh * xl).astype(BF16))
        xb = jnp.exp(eh[0:CHUNK])
        xe = jnp.exp(eh[(N_LEVELS + 1) * CHUNK:(N_LEVELS + 2) * CHUNK])
        st = st_ref[d, h]
        o = jnp.dot(att.astype(BF16), vh, preferred_element_type=F32)
        o = o + _dot_nt((qh * xb).astype(BF16), st.astype(BF16))
        st_ref[d, h] = st * xb[edge_row:edge_row + 1, :] + _dot_tn(vh, (kh * xe).astype(BF16))
        outs.append(o)
    return jnp.concatenate(outs, axis=1)


def _scan_loop(n_chunks, body):
    def step(c, carry):
        body(c)
        return carry
    lax.fori_loop(0, n_chunks, step, 0)


def _gla_scan_kernel(qf_ref, kf_ref, vf_ref, sf_ref, qb_ref, kb_ref, vb_ref, sb_ref,
                     wg_ref, bg_ref, cm_ref, mask_ref, s0_ref,
                     of_ref, ob_ref, st_ref):
    @pl.when(pl.program_id(0) == 0)
    def _():
        st_ref[...] = s0_ref[...]

    n_chunks = qf_ref.shape[0] // CHUNK

    def gate(small, d):
        glr = small[:, 2 * SSD_HEADS + d * GLA_RANK:2 * SSD_HEADS + (d + 1) * GLA_RANK]
        logit = jnp.dot(glr.astype(BF16), wg_ref[d], preferred_element_type=F32) + bg_ref[d]
        return _log_sigmoid(logit) / GLA_NORMALIZER

    def body(c):
        rf = pl.multiple_of(c * CHUNK, CHUNK)
        rows = pl.ds(rf, CHUNK)
        of_ref[rows, :] = _gla_chunk(qf_ref[rows, :], kf_ref[rows, :], vf_ref[rows, :],
                                     gate(sf_ref[rows, :], 0), st_ref, 0, cm_ref, mask_ref)
        rb = pl.multiple_of((n_chunks - 1 - c) * CHUNK, CHUNK)
        rows = pl.ds(rb, CHUNK)
        ob_ref[rows, :] = _gla_chunk(qb_ref[rows, :], kb_ref[rows, :], vb_ref[rows, :],
                                     gate(sb_ref[rows, :], 1), st_ref, 1, cm_ref, mask_ref)

    _scan_loop(n_chunks, body)


def _hg_scan_kernel(qf_ref, ff_ref, vf_ref, qb_ref, fb_ref, vb_ref,
                    lb_ref, cm_ref, mask_ref, s0_ref,
                    of_ref, ob_ref, st_ref):
    @pl.when(pl.program_id(0) == 0)
    def _():
        st_ref[...] = s0_ref[...]

    n_chunks = qf_ref.shape[0] // CHUNK
    lb = lb_ref[...]
    log_lb = jnp.log(lb)
    log1m_lb = jnp.log1p(-lb)
    one_m_lb = 1.0 - lb

    def gates(hf):
        e = jnp.exp(-jnp.abs(hf))
        log_sig = jnp.minimum(hf, 0.0) - jnp.log1p(e)
        t = log1m_lb + log_sig
        m = jnp.maximum(log_lb, t)
        log_f = m + jnp.log1p(jnp.exp(-jnp.abs(log_lb - t)))
        k = one_m_lb * (jnp.where(hf >= 0.0, e, 1.0) / (1.0 + e))
        return k, log_f

    def body(c):
        rf = pl.multiple_of(c * CHUNK, CHUNK)
        rows = pl.ds(rf, CHUNK)
        k, g = gates(ff_ref[rows, :])
        of_ref[rows, :] = _gla_chunk(qf_ref[rows, :], k, vf_ref[rows, :], g, st_ref, 0, cm_ref, mask_ref)
        rb = pl.multiple_of((n_chunks - 1 - c) * CHUNK, CHUNK)
        rows = pl.ds(rb, CHUNK)
        k, g = gates(fb_ref[rows, :])
        ob_ref[rows, :] = _gla_chunk(qb_ref[rows, :], k, vb_ref[rows, :], g, st_ref, 1, cm_ref, mask_ref)

    _scan_loop(n_chunks, body)


def _const_spec(shape):
    nd = len(shape)
    return pl.BlockSpec(shape, lambda i: (0,) * nd)


def _scan_specs(t):
    tb = min(TB_SCAN, t)
    nb = t // tb
    fwd = lambda w, j: pl.BlockSpec((tb, w), lambda i: (i, j))
    bwd = lambda w, j: pl.BlockSpec((tb, w), lambda i: (nb - 1 - i, j))
    return tb, nb, fwd, bwd


def _gla_scan(gqk, gv, small, wg, bg, cm, mask, s0):
    t = gqk.shape[0]
    tb, nb, fwd, bwd = _scan_specs(t)
    st_shape = (2, HEADS, HEAD_W, HEAD_W)
    return pl.pallas_call(
        _gla_scan_kernel,
        grid=(nb,),
        in_specs=[fwd(MIXW, 0), fwd(MIXW, 1), fwd(MIXW, 0), fwd(128, 0),
                  bwd(MIXW, 0), bwd(MIXW, 1), bwd(MIXW, 0), bwd(128, 0),
                  _const_spec(wg.shape), _const_spec(bg.shape), _const_spec(cm.shape), _const_spec(mask.shape),
                  _const_spec(st_shape)],
        out_specs=[fwd(MIXW, 0), bwd(MIXW, 0), _const_spec(st_shape)],
        out_shape=[jax.ShapeDtypeStruct((t, MIXW), F32), jax.ShapeDtypeStruct((t, MIXW), F32),
                   jax.ShapeDtypeStruct(st_shape, F32)],
        compiler_params=pltpu.CompilerParams(dimension_semantics=("arbitrary",), vmem_limit_bytes=VMEM_LIMIT),
        name="gla_scan",
    )(gqk, gqk, gv, small, gqk, gqk, gv, small, wg, bg, cm, mask, s0)


def _hg_scan(hq, hf, hi, lb, cm, mask, s0):
    t = hq.shape[0]
    tb, nb, fwd, bwd = _scan_specs(t)
    st_shape = (2, HEADS, HEAD_W, HEAD_W)
    return pl.pallas_call(
        _hg_scan_kernel,
        grid=(nb,),
        in_specs=[fwd(MIXW, 0), fwd(MIXW, 0), fwd(MIXW, 0),
                  bwd(MIXW, 0), bwd(MIXW, 1), bwd(MIXW, 0),
                  _const_spec(lb.shape), _const_spec(cm.shape), _const_spec(mask.shape), _const_spec(st_shape)],
        out_specs=[fwd(MIXW, 0), bwd(MIXW, 0), _const_spec(st_shape)],
        out_shape=[jax.ShapeDtypeStruct((t, MIXW), F32), jax.ShapeDtypeStruct((t, MIXW), F32),
                   jax.ShapeDtypeStruct(st_shape, F32)],
        compiler_params=pltpu.CompilerParams(dimension_semantics=("arbitrary",), vmem_limit_bytes=VMEM_LIMIT),
        name="hg_scan",
    )(hq, hf, hi, hq, hf, hi, lb, cm, mask, s0)


def _ssd_chunk(xs, bc, dt_raw, st_ref, d, dtb_ref, alog_ref, cm_ref, ex_ref, diag_ref, causal_ref):
    dt = dt_raw + dtb_ref[d]
    dt = jnp.maximum(dt, 0.0) + jnp.log1p(jnp.exp(-jnp.abs(dt)))
    a = dt * (-jnp.exp(alog_ref[d]))
    ce = _exact_left_mul(cm_ref[d], a)
    acs, aed = ce[0:CHUNK], ce[CHUNK:2 * CHUNK]
    small = jnp.concatenate([dt, acs, jnp.exp(acs), jnp.exp(aed)], axis=0)
    hi, mid, lo = _split3(small)
    ex = ex_ref[...]
    big = (jnp.dot(hi, ex, preferred_element_type=F32) + jnp.dot(mid, ex, preferred_element_type=F32)
           + jnp.dot(lo, ex, preferred_element_type=F32))
    dt_x, acs_x = big[0:CHUNK], big[CHUNK:2 * CHUNK]
    dec_x, edge_x = big[2 * CHUNK:3 * CHUNK], big[3 * CHUNK:4 * CHUNK]
    acs_row = jnp.sum(acs_x * diag_ref[...], axis=0, keepdims=True)
    lmat = jnp.exp(jnp.where(causal_ref[d] > 0.0, acs_x - acs_row, -jnp.inf))
    xdt = xs * dt_x
    xed = (xdt * edge_x).astype(BF16)
    lane = lax.broadcasted_iota(jnp.int32, (CHUNK, 128), 1)
    first = lane < SSD_HEADDIM
    edge_row = (CHUNK - 1) if d == 0 else 0
    gw = SSD_INNER // SSD_GROUPS
    outs = []
    for grp in range(SSD_GROUPS):
        b_g = bc[:, grp * SSD_STATE:(grp + 1) * SSD_STATE]
        c_g = bc[:, (SSD_GROUPS + grp) * SSD_STATE:(SSD_GROUPS + grp + 1) * SSD_STATE]
        scores2 = _dot_nt(c_g, jnp.concatenate([b_g, b_g], axis=0))
        st = st_ref[d, :, grp * gw:(grp + 1) * gw]
        y_off = jnp.dot(c_g, st.astype(BF16), preferred_element_type=F32) * dec_x[:, grp * gw:(grp + 1) * gw]
        for j in range(gw // 128):
            lo_l = grp * gw + j * 128
            w = (scores2 * lmat[:, lo_l:lo_l + 128]).astype(BF16)
            xj = xdt[:, lo_l:lo_l + 128]
            xm = jnp.concatenate([jnp.where(first, xj, 0.0), jnp.where(first, 0.0, xj)], axis=0).astype(BF16)
            outs.append(jnp.dot(w, xm, preferred_element_type=F32) + y_off[:, j * 128:(j + 1) * 128])
        st_ref[d, :, grp * gw:(grp + 1) * gw] = (
            st * dec_x[edge_row:edge_row + 1, grp * gw:(grp + 1) * gw]
            + _dot_tn(b_g, xed[:, grp * gw:(grp + 1) * gw]))
    return jnp.concatenate(outs, axis=1)


def _ssd_scan_kernel(xf_ref, bcf_ref, sf_ref, xb_ref, bcb_ref, sb_ref,
                     dtb_ref, alog_ref, cm_ref, ex_ref, diag_ref, causal_ref, s0_ref,
                     of_ref, ob_ref, st_ref):
    @pl.when(pl.program_id(0) == 0)
    def _():
        st_ref[...] = s0_ref[...]

    n_chunks = xf_ref.shape[0] // CHUNK

    def body(c):
        rf = pl.multiple_of(c * CHUNK, CHUNK)
        rows = pl.ds(rf, CHUNK)
        of_ref[rows, :] = _ssd_chunk(xf_ref[rows, :], bcf_ref[rows, :], sf_ref[rows, :][:, 0:SSD_HEADS],
                                     st_ref, 0, dtb_ref, alog_ref, cm_ref, ex_ref, diag_ref, causal_ref)
        rb = pl.multiple_of((n_chunks - 1 - c) * CHUNK, CHUNK)
        rows = pl.ds(rb, CHUNK)
        ob_ref[rows, :] = _ssd_chunk(xb_ref[rows, :], bcb_ref[rows, :],
                                     sb_ref[rows, :][:, SSD_HEADS:2 * SSD_HEADS],
                                     st_ref, 1, dtb_ref, alog_ref, cm_ref, ex_ref, diag_ref, causal_ref)

    _scan_loop(n_chunks, body)


def _ssd_scan(xs, bc, small, dt_bias, a_log, cm, expand, diag, causal, s0):
    t = xs.shape[0]
    tb, nb, fwd, bwd = _scan_specs(t)
    st_shape = (2, SSD_STATE, SSD_INNER)
    return pl.pallas_call(
        _ssd_scan_kernel,
        grid=(nb,),
        in_specs=[fwd(SSD_INNER, 0), fwd(512, 0), fwd(128, 0),
                  bwd(SSD_INNER, 0), bwd(512, 0), bwd(128, 0),
                  _const_spec(dt_bias.shape), _const_spec(a_log.shape), _const_spec(cm.shape),
                  _const_spec(expand.shape), _const_spec(diag.shape), _const_spec(causal.shape),
                  _const_spec(st_shape)],
        out_specs=[fwd(SSD_INNER, 0), bwd(SSD_INNER, 0), _const_spec(st_shape)],
        out_shape=[jax.ShapeDtypeStruct((t, SSD_INNER), F32), jax.ShapeDtypeStruct((t, SSD_INNER), F32),
                   jax.ShapeDtypeStruct(st_shape, F32)],
        compiler_params=pltpu.CompilerParams(dimension_semantics=("arbitrary",), vmem_limit_bytes=VMEM_LIMIT),
        name="ssd_scan",
    )(xs, bc, small, xs, bc, small, dt_bias, a_log, cm, expand, diag, causal, s0)


def _out_kernel(x_ref, ysf_ref, ysb_ref, xs_ref, z_ref, ygf_ref, ygb_ref, gg_ref, yhf_ref, yhb_ref, hgate_ref,
                dskip_ref, sng_ref, gng_ref, hng_ref, wo_ref, g1_ref, n2g_ref, sh2_ref, sc2_ref, g2_ref,
                w1_ref, w2_ref, fng_ref, o_ref, *, final_norm):
    y_s = ysf_ref[...] + ysb_ref[...] + dskip_ref[...] * xs_ref[...]
    y_s = _rms(y_s * _silu(z_ref[...]), sng_ref[...]).astype(BF16)
    acc = jnp.dot(y_s, wo_ref[0:SSD_INNER, :], preferred_element_type=F32)
    y_g = ygf_ref[...] + ygb_ref[...]
    y_h = yhf_ref[...] + yhb_ref[...]
    gate_g = _silu(gg_ref[...])
    gate_h = _sigmoid(hgate_ref[...])
    for h in range(HEADS):
        sl = slice(h * HEAD_W, (h + 1) * HEAD_W)
        yg = (_rms(y_g[:, sl], gng_ref[...]) * gate_g[:, sl]).astype(BF16)
        yh = (_rms(y_h[:, sl], hng_ref[...]) * gate_h[:, sl]).astype(BF16)
        acc = acc + jnp.dot(yg, wo_ref[SSD_INNER + h * HEAD_W:SSD_INNER + (h + 1) * HEAD_W, :],
                            preferred_element_type=F32)
        acc = acc + jnp.dot(yh, wo_ref[SSD_INNER + MIXW + h * HEAD_W:SSD_INNER + MIXW + (h + 1) * HEAD_W, :],
                            preferred_element_type=F32)
    x1 = x_ref[...] + g1_ref[...] * acc
    hn = (_rms(x1, n2g_ref[...]) * (1.0 + sc2_ref[...]) + sh2_ref[...]).astype(BF16)
    m = jnp.zeros_like(x1)
    fb = 1024
    for j in range(FF // fb):
        a = jnp.maximum(jnp.dot(hn, w1_ref[:, j * fb:(j + 1) * fb], preferred_element_type=F32), 0.0)
        m = m + jnp.dot((a * a).astype(BF16), w2_ref[j * fb:(j + 1) * fb, :], preferred_element_type=F32)
    x2 = x1 + g2_ref[...] * m
    if final_norm:
        x2 = _rms(x2, fng_ref[...])
    o_ref[...] = x2


def _out_block(x2, ysf, ysb, xs, z, ygf, ygb, gg, yhf, yhb, hgate,
               dskip, sng, gng, hng, wo, g1, n2g, sh2, sc2, g2, w1, w2, fng, final_norm):
    t = x2.shape[0]
    tm = min(TM_PROJ, t)
    tile = lambda w: pl.BlockSpec((tm, w), lambda i: (i, 0))
    row = lambda w: pl.BlockSpec((1, w), lambda i: (0, 0))
    resident = lambda shape: pl.BlockSpec(shape, lambda i: (0, 0), pipeline_mode=pl.Buffered(1))
    return pl.pallas_call(
        functools.partial(_out_kernel, final_norm=final_norm),
        grid=(t // tm,),
        in_specs=[tile(D), tile(D), tile(D), tile(D), tile(D),
                  tile(MIXW), tile(MIXW), tile(MIXW), tile(MIXW), tile(MIXW), tile(MIXW),
                  row(D), row(D), row(HEAD_W), row(HEAD_W), resident(wo.shape),
                  row(D), row(D), row(D), row(D), row(D),
                  resident(w1.shape), resident(w2.shape), row(D)],
        out_specs=tile(D),
        out_shape=jax.ShapeDtypeStruct((t, D), F32),
        compiler_params=pltpu.CompilerParams(dimension_semantics=("arbitrary",), vmem_limit_bytes=VMEM_LIMIT),
        name="out_block",
    )(x2, ysf, ysb, xs, z, ygf, ygb, gg, yhf, yhb, hgate,
      dskip, sng, gng, hng, wo, g1, n2g, sh2, sc2, g2, w1, w2, fng)


def _pad_heads(w, heads, width):
    lead = w.shape[:-1]
    w = w.reshape(lead + (heads, width))
    w = jnp.pad(w, [(0, 0)] * len(lead) + [(0, 0), (0, HEAD_W - width)])
    return w.reshape(lead + (heads * HEAD_W,))


def _permute_w_in(w):
    o = _IN_OFFS
    col = lambda i: w[:, o[i]:o[i + 1]]
    parts = [col(0), col(1),
             _pad_heads(col(3), GLA_HEADS, GLA_HEAD_K), _pad_heads(col(4), GLA_HEADS, GLA_HEAD_K),
             col(5), col(6), col(8), col(9), col(10), col(11),
             col(2), col(7), jnp.zeros((w.shape[0], 64), w.dtype)]
    return jnp.concatenate(parts, axis=1).astype(BF16)


def kernel(x, c, ctx, c_ctx, norm1_g, norm2_g, w_mod, b_mod, w_in, ssd_conv_w, ssd_conv_b, ssd_dt_bias,
           ssd_a_log, ssd_d, ssd_norm_g, gla_w_gk2, gla_b_gk, gla_norm_g, hg_lb_logits, hg_norm_g, w_out,
           w_mlp1, w_mlp2, final_norm_g):
    depth = w_in.shape[0]
    assert x.shape[0] == 1 and c.shape[0] == 1 and ctx.shape[0] == 1
    seq, ctx_len = x.shape[1], ctx.shape[1]
    assert seq % TB_SCAN == 0 and ctx_len % CHUNK == 0 and ctx_len <= TB_SCAN

    cm = jnp.asarray(_CM_NP, BF16)
    mask = jnp.asarray(_MASK_NP, F32)
    ssd_cm = jnp.asarray(_SSD_CM_NP, BF16)
    expand = jnp.asarray(_EXPAND_NP, BF16)
    diag = jnp.asarray(_DIAG_NP, F32)
    causal = jnp.asarray(_CAUSAL_NP, F32)

    lbs = jnp.cumsum(jax.nn.softmax(hg_lb_logits.astype(F32), axis=0), axis=0)
    lbs = lbs - lbs[0]

    cc = jnp.zeros((8, D), F32).at[0].set(c[0]).at[1].set(c_ctx)
    mod = _modulation(cc, w_mod, b_mod)

    row = lambda v: v.reshape(1, -1)
    xl = x[0]
    xc = ctx[0]
    zeros_g = jnp.zeros((2, HEADS, HEAD_W, HEAD_W), F32)
    zeros_s = jnp.zeros((2, SSD_STATE, SSD_INNER), F32)
    for layer in range(depth):
        last = layer == depth - 1
        w_p = _permute_w_in(w_in[layer])
        wo = w_out[layer].astype(BF16)
        w1 = w_mlp1[layer].astype(BF16)
        w2 = w_mlp2[layer].astype(BF16)
        wg = _pad_heads(gla_w_gk2[layer], GLA_HEADS, GLA_HEAD_K).astype(BF16)
        bg = _pad_heads(gla_b_gk[layer], GLA_HEADS, GLA_HEAD_K).reshape(2, 1, MIXW)
        dtb = ssd_dt_bias[layer].reshape(2, 1, SSD_HEADS)
        alog = ssd_a_log[layer].reshape(2, 1, SSD_HEADS)
        lb = row(lbs[layer])
        dskip = row(jnp.repeat(ssd_d[layer], SSD_HEADDIM))
        gng, hng = row(gla_norm_g[layer]), row(hg_norm_g[layer])
        sng, n1g, n2g = row(ssd_norm_g[layer]), row(norm1_g[layer]), row(norm2_g[layer])
        fng = row(final_norm_g)
        cw, cb = ssd_conv_w[layer], row(ssd_conv_b[layer])

        def mods(r):
            return [mod[layer, r:r + 1, i * D:(i + 1) * D] for i in range(6)]

        def mix(xin, m, row_len, s_ssd, s_gla, s_hg):
            sh1, sc1 = m[0], m[1]
            z, xs, bc, gqk, gv, gg, hq, hf, hi, hgate, small = _inproj(xin, n1g, sh1, sc1, w_p, cw, cb, row_len)
            ysf, ysb, st_s = _ssd_scan(xs, bc, small, dtb, alog, ssd_cm, expand, diag, causal, s_ssd)
            ygf, ygb, st_g = _gla_scan(gqk, gv, small, wg, bg, cm, mask, s_gla)
            yhf, yhb, st_h = _hg_scan(hq, hf, hi, lb, cm, mask, s_hg)
            return (ysf, ysb, xs, z, ygf, ygb, gg, yhf, yhb, hgate), (st_s, st_g, st_h)

        def finish(xin, ys, m, final_norm):
            return _out_block(xin, *ys, dskip, sng, gng, hng, wo, m[2], n2g, m[3], m[4], m[5], w1, w2, fng,
                              final_norm)

        m_ctx, m_lat = mods(1), mods(0)
        ys_ctx, states = mix(xc, m_ctx, ctx_len, zeros_s, zeros_g, zeros_g)
        ys_lat, _ = mix(xl, m_lat, CHUNK, *states)
        xl = finish(xl, ys_lat, m_lat, last)
        if not last:
            xc = finish(xc, ys_ctx, m_ctx, False)
    return xl[None]
```

```python
import functools

import numpy as np
import jax
import jax.numpy as jnp
from jax import lax
from jax.experimental import pallas as pl
from jax.experimental.pallas import tpu as pltpu

F32 = jnp.float32
BF16 = jnp.bfloat16

D = 1024
CHUNK = 64
EPS = 1e-6
SSD_INNER = 1024
SSD_HEADS = 16
SSD_HEADDIM = 64
SSD_STATE = 128
SSD_GROUPS = 2
SSD_CONV = 5
SSD_CONV_DIM = 1536
GLA_HEADS = 4
GLA_HEAD_K = 64
GLA_RANK = 16
GLA_NORMALIZER = 16.0
HEADS = 4
HEAD_W = 128
MIXW = HEADS * HEAD_W
PAIR_W = 2 * HEAD_W
DT_REP = 3
GLR_LANE = 2 * DT_REP * SSD_HEADS
FF = 4096
N_LEVELS = 6

VMEM_LIMIT = 56 * 1024 * 1024
TM_PROJ = 256
TB_SCAN = 256

_IN_SIZES = (1024, 1536, 32, 256, 256, 512, 512, 32, 512, 1024, 512, 512)
_IN_OFFS = np.concatenate([[0], np.cumsum(_IN_SIZES)]).tolist()

_P_Z, _P_XBC, _P_GQ, _P_GK, _P_GV, _P_GG, _P_HQ, _P_HF, _P_HI, _P_HGATE, _P_SMALL, _P_END = (
    0, 1024, 2560, 3072, 3584, 4096, 4608, 5120, 6144, 6656, 7168, 7296)


def _chunk_constants():
    n = CHUNK
    p = np.arange(n)[:, None]
    r = np.arange(n)[None, :]
    blocks = [(r <= p)]
    masks = [(p == r)]
    for lev in range(1, N_LEVELS + 1):
        size = 1 << lev
        half = size // 2
        start = (p // size) * size
        mid = start + half - 1
        second = (p - start) >= half
        m = np.where(second, (r > mid) & (r <= p), (r > p) & (r <= mid))
        blocks.append(m)
        same = (p // size) == (r // size)
        masks.append(same & second & ((r - (r // size) * size) < half))
    blocks.append(r > p)
    fwd = np.concatenate([b.astype(np.float32) for b in blocks], axis=0)
    fmask = np.stack([m.astype(np.float32) for m in masks], axis=0)
    bwd = np.concatenate([b.astype(np.float32)[::-1, ::-1] for b in blocks], axis=0)
    bmask = fmask[:, ::-1, ::-1]
    cm = np.stack([fwd, bwd], axis=0)
    cm = np.concatenate([cm, cm, cm], axis=2)
    mk = np.stack([fmask, bmask], axis=0)
    mk = np.concatenate([mk, mk], axis=3)
    return cm, np.ascontiguousarray(mk)


def _ssd_constants():
    n = CHUNK
    p = np.arange(n)[:, None]
    r = np.arange(n)[None, :]
    cum_f = (r <= p).astype(np.float32)
    edge_f = (r > p).astype(np.float32)
    cm = np.stack([np.concatenate([cum_f, edge_f], 0),
                   np.concatenate([cum_f[::-1, ::-1], edge_f[::-1, ::-1]], 0)], 0)
    cm = np.concatenate([cm, cm, cm], axis=2)
    lanes = np.arange(SSD_INNER)[None, :]
    head_rows = (np.arange(SSD_HEADS)[:, None] == lanes // SSD_HEADDIM).astype(np.float32)
    expand = np.zeros((2, 128, SSD_INNER), np.float32)
    part = np.full((2, 1, 128), 2, np.int32)
    for d in range(2):
        for rep in range(3):
            lo = DT_REP * SSD_HEADS * d + rep * SSD_HEADS
            expand[d, lo:lo + SSD_HEADS] = head_rows
            part[d, 0, lo:lo + SSD_HEADS] = rep
    s_of_lane = lanes % SSD_HEADDIM
    diag = (p == s_of_lane).astype(np.float32)
    causal = np.stack([(p >= s_of_lane), (p <= s_of_lane)], 0).astype(np.float32)
    return cm, expand, part, diag, causal


_CM_NP, _MASK_NP = _chunk_constants()
_SSD_CM_NP, _EXPAND_NP, _PART_NP, _DIAG_NP, _CAUSAL_NP = _ssd_constants()


def _sigmoid(x):
    return 1.0 / (1.0 + jnp.exp(-x))


def _silu(x):
    return x * _sigmoid(x)


def _log_sigmoid(x):
    return jnp.minimum(x, 0.0) - jnp.log1p(jnp.exp(-jnp.abs(x)))


def _split3(x):
    hi = x.astype(BF16)
    r1 = x - hi.astype(F32)
    mid = r1.astype(BF16)
    r2 = r1 - mid.astype(F32)
    lo = r2.astype(BF16)
    return hi, mid, lo


def _exact_left_mul(m01, x):
    hi, mid, lo = _split3(x)
    return (jnp.dot(m01, hi, preferred_element_type=F32)
            + jnp.dot(m01, mid, preferred_element_type=F32)
            + jnp.dot(m01, lo, preferred_element_type=F32))


def _dot_nt(a, b):
    return lax.dot_general(a, b, (((1,), (1,)), ((), ())), preferred_element_type=F32)


def _dot_tn(a, b):
    return lax.dot_general(a, b, (((0,), (0,)), ((), ())), preferred_element_type=F32)


def _rms(x, gain):
    return x * lax.rsqrt(jnp.mean(x * x, axis=-1, keepdims=True) + EPS) * gain


def _mod_kernel(cc_ref, w_ref, b_ref, o_ref):
    s = _silu(cc_ref[...])
    o_ref[0] = jnp.dot(s, w_ref[0], precision=lax.Precision.HIGHEST, preferred_element_type=F32) + b_ref[0]


def _modulation(cc, w_mod, b_mod):
    depth, _, n_mod = w_mod.shape
    bn = 1536
    return pl.pallas_call(
        _mod_kernel,
        grid=(depth, n_mod // bn),
        in_specs=[pl.BlockSpec((8, D), lambda l, j: (0, 0)),
                  pl.BlockSpec((1, D, bn), lambda l, j: (l, 0, j)),
                  pl.BlockSpec((1, 1, bn), lambda l, j: (l, 0, j))],
        out_specs=pl.BlockSpec((1, 8, bn), lambda l, j: (l, 0, j)),
        out_shape=jax.ShapeDtypeStruct((depth, 8, n_mod), F32),
        compiler_params=pltpu.CompilerParams(dimension_semantics=("arbitrary", "arbitrary"),
                                             vmem_limit_bytes=VMEM_LIMIT),
        name="modulation",
    )(cc, w_mod, b_mod.reshape(depth, 1, n_mod))


def _inproj_kernel(x_ref, g_ref, sh_ref, sc_ref, w_ref, cw_ref, cb_ref,
                   z_ref, xs_ref, bc_ref, gqk_ref, gv_ref, gg_ref, hq_ref, hf_ref, hi_ref, hgate_ref,
                   small_ref, *, row_len):
    x = x_ref[...]
    tm = x.shape[0]
    h = _rms(x, g_ref[...]) * (1.0 + sc_ref[...]) + sh_ref[...]
    hb = h.astype(BF16)

    def proj(a, b):
        return jnp.dot(hb, w_ref[:, a:b], preferred_element_type=F32)

    z_ref[...] = proj(_P_Z, _P_XBC)

    u = proj(_P_XBC, _P_GQ)
    pos = lax.broadcasted_iota(jnp.int32, (tm, 128), 0) & (row_len - 1)
    shifts = [k - SSD_CONV // 2 for k in range(SSD_CONV)]
    valid = {s: ((pos + s >= 0) & (pos + s < row_len)) for s in shifts if s != 0}
    for j in range(SSD_CONV_DIM // 128):
        lo = j * 128
        uj = u[:, lo:lo + 128]
        acc = uj * cw_ref[SSD_CONV // 2:SSD_CONV // 2 + 1, lo:lo + 128] + cb_ref[:, lo:lo + 128]
        for k, s in enumerate(shifts):
            if s == 0:
                continue
            us = pltpu.roll(uj, (-s) % tm, axis=0)
            acc = acc + jnp.where(valid[s], us, 0.0) * cw_ref[k:k + 1, lo:lo + 128]
        v = _silu(acc)
        if lo < SSD_INNER:
            xs_ref[:, lo:lo + 128] = v
        else:
            bc_ref[:, lo - SSD_INNER:lo - SSD_INNER + 128] = v.astype(BF16)

    q = proj(_P_GQ, _P_GK)
    gqk_ref[:, 0:MIXW] = q * (GLA_HEAD_K ** -0.5)
    gqk_ref[:, MIXW:2 * MIXW] = proj(_P_GK, _P_GV)
    gv_ref[...] = proj(_P_GV, _P_GG).astype(BF16)
    gg_ref[...] = proj(_P_GG, _P_HQ)
    hq_ref[...] = _silu(proj(_P_HQ, _P_HF))
    hf_ref[...] = proj(_P_HF, _P_HI)
    hi_ref[...] = proj(_P_HI, _P_HGATE).astype(BF16)
    hgate_ref[...] = proj(_P_HGATE, _P_SMALL)
    small_ref[...] = proj(_P_SMALL, _P_END)


def _inproj(x2, gain, shift, scale, w_p, conv_w, conv_b, row_len):
    t = x2.shape[0]
    tm = min(TM_PROJ, t)
    row = lambda w: pl.BlockSpec((1, w), lambda i: (0, 0))
    tile = lambda w: pl.BlockSpec((tm, w), lambda i: (i, 0))
    outs = [(D, F32), (SSD_INNER, F32), (512, BF16), (2 * MIXW, F32), (MIXW, BF16), (MIXW, F32),
            (MIXW, F32), (2 * MIXW, F32), (MIXW, BF16), (MIXW, F32), (128, F32)]
    return pl.pallas_call(
        functools.partial(_inproj_kernel, row_len=row_len),
        grid=(t // tm,),
        in_specs=[tile(D), row(D), row(D), row(D),
                  pl.BlockSpec((D, _P_END), lambda i: (0, 0), pipeline_mode=pl.Buffered(1)),
                  pl.BlockSpec((SSD_CONV, SSD_CONV_DIM), lambda i: (0, 0)),
                  row(SSD_CONV_DIM)],
        out_specs=[tile(w) for w, _ in outs],
        out_shape=[jax.ShapeDtypeStruct((t, w), dt) for w, dt in outs],
        compiler_params=pltpu.CompilerParams(dimension_semantics=("arbitrary",), vmem_limit_bytes=VMEM_LIMIT),
        name="inproj",
    )(x2, gain, shift, scale, w_p, conv_w, conv_b)


def _pair_block_diag(a):
    z = jnp.zeros((a.shape[0], HEAD_W), a.dtype)
    return jnp.concatenate([jnp.concatenate([a[:, :HEAD_W], z], axis=1),
                            jnp.concatenate([z, a[:, HEAD_W:]], axis=1)], axis=0)


def _gla_chunks(streams, st_ref, cm_ref, mask_ref):
    def blk(x, i):
        return x[i * CHUNK:(i + 1) * CHUNK]

    xs = []
    for (_, _, _, g, d) in streams:
        e = jnp.dot(cm_ref[d], jnp.concatenate(_split3(g), axis=0), preferred_element_type=F32)
        xs.append(jnp.exp(e))

    atts = []
    for (q, k, _, _, d), x in zip(streams, xs):
        for p in range(HEADS // 2):
            sl = slice(p * PAIR_W, (p + 1) * PAIR_W)
            qp, kp = q[:, sl], k[:, sl]
            att = mask_ref[d, 0] * _dot_nt(qp.astype(BF16), _pair_block_diag(kp.astype(BF16)))
            for lev in range(1, N_LEVELS + 1):
                xl = blk(x, lev)[:, sl]
                att = att + mask_ref[d, lev] * _dot_nt((qp * xl).astype(BF16),
                                                       _pair_block_diag((kp * xl).astype(BF16)))
            atts.append(att)

    outs = []
    zs = jnp.zeros((HEAD_W, HEAD_W), BF16)
    for si, ((q, k, v, _, d), x) in enumerate(zip(streams, xs)):
        edge_row = (CHUNK - 1) if d == 0 else 0
        xb = blk(x, 0)
        xe = blk(x, N_LEVELS + 1)
        parts = []
        for p in range(HEADS // 2):
            sl = slice(p * PAIR_W, (p + 1) * PAIR_W)
            vp = v[:, sl]
            st0, st1 = st_ref[d, 2 * p], st_ref[d, 2 * p + 1]
            st_bd = jnp.concatenate([jnp.concatenate([st0.astype(BF16), zs], axis=1),
                                     jnp.concatenate([zs, st1.astype(BF16)], axis=1)], axis=0)
            o = jnp.dot(atts[si * (HEADS // 2) + p].astype(BF16), _pair_block_diag(vp),
                        preferred_element_type=F32)
            o = o + _dot_nt((q[:, sl] * xb[:, sl]).astype(BF16), st_bd)
            kx = (k[:, sl] * xe[:, sl]).astype(BF16)
            decay = xb[edge_row:edge_row + 1, sl]
            for i, st in enumerate((st0, st1)):
                hs = slice(i * HEAD_W, (i + 1) * HEAD_W)
                st_ref[d, 2 * p + i] = st * decay[:, hs] + _dot_tn(vp[:, hs], kx[:, hs])
            parts.append(o)
        outs.append(jnp.concatenate(parts, axis=1))
    return outs


def _scan_loop(n_chunks, body):
    def step(c, carry):
        body(c)
        return carry
    lax.fori_loop(0, n_chunks, step, 0)


def _gla_scan_kernel(qf_ref, kf_ref, vf_ref, sf_ref, qb_ref, kb_ref, vb_ref, sb_ref,
                     wg_ref, bg_ref, cm_ref, mask_ref, s0_ref,
                     of_ref, ob_ref, st_ref):
    @pl.when(pl.program_id(0) == 0)
    def _():
        st_ref[...] = s0_ref[...]

    n_chunks = qf_ref.shape[0] // CHUNK

    def gate(small, d):
        logit = jnp.dot(small.astype(BF16), wg_ref[d], preferred_element_type=F32) + bg_ref[d]
        return _log_sigmoid(logit) / GLA_NORMALIZER

    def body(c):
        rf = pl.ds(pl.multiple_of(c * CHUNK, CHUNK), CHUNK)
        rb = pl.ds(pl.multiple_of((n_chunks - 1 - c) * CHUNK, CHUNK), CHUNK)
        streams = [(qf_ref[rf, :], kf_ref[rf, :], vf_ref[rf, :], gate(sf_ref[rf, :], 0), 0),
                   (qb_ref[rb, :], kb_ref[rb, :], vb_ref[rb, :], gate(sb_ref[rb, :], 1), 1)]
        of_ref[rf, :], ob_ref[rb, :] = _gla_chunks(streams, st_ref, cm_ref, mask_ref)

    _scan_loop(n_chunks, body)


def _hg_scan_kernel(qf_ref, ff_ref, vf_ref, qb_ref, fb_ref, vb_ref,
                    lb_ref, cm_ref, mask_ref, s0_ref,
                    of_ref, ob_ref, st_ref):
    @pl.when(pl.program_id(0) == 0)
    def _():
        st_ref[...] = s0_ref[...]

    n_chunks = qf_ref.shape[0] // CHUNK
    lb = lb_ref[...]
    log_lb = jnp.log(lb)
    log1m_lb = jnp.log1p(-lb)
    one_m_lb = 1.0 - lb

    def gates(hf):
        e = jnp.exp(-jnp.abs(hf))
        log_sig = jnp.minimum(hf, 0.0) - jnp.log1p(e)
        t = log1m_lb + log_sig
        m = jnp.maximum(log_lb, t)
        log_f = m + jnp.log1p(jnp.exp(-jnp.abs(log_lb - t)))
        k = one_m_lb * (jnp.where(hf >= 0.0, e, 1.0) / (1.0 + e))
        return k, log_f

    def body(c):
        rf = pl.ds(pl.multiple_of(c * CHUNK, CHUNK), CHUNK)
        rb = pl.ds(pl.multiple_of((n_chunks - 1 - c) * CHUNK, CHUNK), CHUNK)
        kf, gf = gates(ff_ref[rf, :])
        kb, gb = gates(fb_ref[rb, :])
        streams = [(qf_ref[rf, :], kf, vf_ref[rf, :], gf, 0), (qb_ref[rb, :], kb, vb_ref[rb, :], gb, 1)]
        of_ref[rf, :], ob_ref[rb, :] = _gla_chunks(streams, st_ref, cm_ref, mask_ref)

    _scan_loop(n_chunks, body)


def _const_spec(shape):
    nd = len(shape)
    return pl.BlockSpec(shape, lambda i: (0,) * nd)


def _scan_specs(t):
    tb = min(TB_SCAN, t)
    nb = t // tb
    fwd = lambda w, j: pl.BlockSpec((tb, w), lambda i: (i, j))
    bwd = lambda w, j: pl.BlockSpec((tb, w), lambda i: (nb - 1 - i, j))
    return tb, nb, fwd, bwd


def _gla_scan(gqk, gv, small, wg, bg, cm, mask, s0):
    t = gqk.shape[0]
    tb, nb, fwd, bwd = _scan_specs(t)
    st_shape = (2, HEADS, HEAD_W, HEAD_W)
    return pl.pallas_call(
        _gla_scan_kernel,
        grid=(nb,),
        in_specs=[fwd(MIXW, 0), fwd(MIXW, 1), fwd(MIXW, 0), fwd(128, 0),
                  bwd(MIXW, 0), bwd(MIXW, 1), bwd(MIXW, 0), bwd(128, 0),
                  _const_spec(wg.shape), _const_spec(bg.shape), _const_spec(cm.shape), _const_spec(mask.shape),
                  _const_spec(st_shape)],
        out_specs=[fwd(MIXW, 0), bwd(MIXW, 0), _const_spec(st_shape)],
        out_shape=[jax.ShapeDtypeStruct((t, MIXW), F32), jax.ShapeDtypeStruct((t, MIXW), F32),
                   jax.ShapeDtypeStruct(st_shape, F32)],
        compiler_params=pltpu.CompilerParams(dimension_semantics=("arbitrary",), vmem_limit_bytes=VMEM_LIMIT),
        name="gla_scan",
    )(gqk, gqk, gv, small, gqk, gqk, gv, small, wg, bg, cm, mask, s0)


def _hg_scan(hq, hf, hi, lb, cm, mask, s0):
    t = hq.shape[0]
    tb, nb, fwd, bwd = _scan_specs(t)
    st_shape = (2, HEADS, HEAD_W, HEAD_W)
    return pl.pallas_call(
        _hg_scan_kernel,
        grid=(nb,),
        in_specs=[fwd(MIXW, 0), fwd(MIXW, 0), fwd(MIXW, 0),
                  bwd(MIXW, 0), bwd(MIXW, 1), bwd(MIXW, 0),
                  _const_spec(lb.shape), _const_spec(cm.shape), _const_spec(mask.shape), _const_spec(st_shape)],
        out_specs=[fwd(MIXW, 0), bwd(MIXW, 0), _const_spec(st_shape)],
        out_shape=[jax.ShapeDtypeStruct((t, MIXW), F32), jax.ShapeDtypeStruct((t, MIXW), F32),
                   jax.ShapeDtypeStruct(st_shape, F32)],
        compiler_params=pltpu.CompilerParams(dimension_semantics=("arbitrary",), vmem_limit_bytes=VMEM_LIMIT),
        name="hg_scan",
    )(hq, hf, hi, hq, hf, hi, lb, cm, mask, s0)


def _ssd_chunks(streams, st_ref, dtb_ref, alog_ref, part_ref, cm_ref, ex_ref, diag_ref, causal_ref):
    gw = SSD_INNER // SSD_GROUPS
    lane = lax.broadcasted_iota(jnp.int32, (CHUNK, 128), 1)
    first = lane < SSD_HEADDIM
    zx = jnp.zeros((128, 128), BF16)

    stage = []
    for (xs, _, small, d) in streams:
        dt = small + dtb_ref[d]
        dt = jnp.maximum(dt, 0.0) + jnp.log1p(jnp.exp(-jnp.abs(dt)))
        a = dt * (-jnp.exp(alog_ref[d]))
        ce = jnp.dot(cm_ref[d], jnp.concatenate(_split3(a), axis=0), preferred_element_type=F32)
        acs, aed = ce[0:CHUNK], ce[CHUNK:2 * CHUNK]
        sm = jnp.concatenate([dt, acs, jnp.exp(acs), jnp.exp(aed)], axis=0)
        r1 = sm - sm.astype(BF16).astype(F32)
        r2 = r1 - r1.astype(BF16).astype(F32)
        part = part_ref[d]
        parts = jnp.where(part == 0, sm, jnp.where(part == 1, r1, r2)).astype(BF16)
        big = jnp.dot(parts, ex_ref[d], preferred_element_type=F32)
        dt_x, acs_x = big[0:CHUNK], big[CHUNK:2 * CHUNK]
        dec_x, edge_x = big[2 * CHUNK:3 * CHUNK], big[3 * CHUNK:4 * CHUNK]
        acs_row = jnp.sum(acs_x * diag_ref[...], axis=0, keepdims=True)
        lmat = jnp.exp(jnp.where(causal_ref[d] > 0.0, acs_x - acs_row, -jnp.inf))
        xdt = xs * dt_x
        xed = (xdt * edge_x).astype(BF16)
        stage.append((lmat, xdt, xed, dec_x))

    outs = []
    for (_, bc, _, d), (lmat, xdt, xed, dec_x) in zip(streams, stage):
        edge_row = (CHUNK - 1) if d == 0 else 0
        parts = []
        for grp in range(SSD_GROUPS):
            gl = slice(grp * gw, (grp + 1) * gw)
            b_g = bc[:, grp * SSD_STATE:(grp + 1) * SSD_STATE]
            c_g = bc[:, (SSD_GROUPS + grp) * SSD_STATE:(SSD_GROUPS + grp + 1) * SSD_STATE]
            scores2 = _dot_nt(c_g, jnp.concatenate([b_g, b_g], axis=0))
            scores4 = jnp.concatenate([scores2, scores2], axis=1)
            st = st_ref[d, :, gl]
            y_off = jnp.dot(c_g, st.astype(BF16), preferred_element_type=F32) * dec_x[:, gl]
            for j in range(gw // 256):
                ll = slice(grp * gw + j * 256, grp * gw + (j + 1) * 256)
                w = (scores4 * lmat[:, ll]).astype(BF16)
                xj = xdt[:, ll]
                xm = []
                for i in range(2):
                    xi = xj[:, i * 128:(i + 1) * 128]
                    xm.append(jnp.concatenate([jnp.where(first, xi, 0.0), jnp.where(first, 0.0, xi)],
                                              axis=0).astype(BF16))
                xbd = jnp.concatenate([jnp.concatenate([xm[0], zx], axis=1),
                                       jnp.concatenate([zx, xm[1]], axis=1)], axis=0)
                parts.append(jnp.dot(w, xbd, preferred_element_type=F32) + y_off[:, j * 256:(j + 1) * 256])
            st_ref[d, :, gl] = st * dec_x[edge_row:edge_row + 1, gl] + _dot_tn(b_g, xed[:, gl])
        outs.append(jnp.concatenate(parts, axis=1))
    return outs


def _ssd_scan_kernel(xf_ref, bcf_ref, sf_ref, xb_ref, bcb_ref, sb_ref,
                     dtb_ref, alog_ref, part_ref, cm_ref, ex_ref, diag_ref, causal_ref, s0_ref,
                     of_ref, ob_ref, st_ref):
    @pl.when(pl.program_id(0) == 0)
    def _():
        st_ref[...] = s0_ref[...]

    n_chunks = xf_ref.shape[0] // CHUNK

    def body(c):
        rf = pl.ds(pl.multiple_of(c * CHUNK, CHUNK), CHUNK)
        rb = pl.ds(pl.multiple_of((n_chunks - 1 - c) * CHUNK, CHUNK), CHUNK)
        streams = [(xf_ref[rf, :], bcf_ref[rf, :], sf_ref[rf, :], 0),
                   (xb_ref[rb, :], bcb_ref[rb, :], sb_ref[rb, :], 1)]
        of_ref[rf, :], ob_ref[rb, :] = _ssd_chunks(streams, st_ref, dtb_ref, alog_ref, part_ref, cm_ref,
                                                   ex_ref, diag_ref, causal_ref)

    _scan_loop(n_chunks, body)


def _ssd_scan(xs, bc, small, dt_bias, a_log, part, cm, expand, diag, causal, s0):
    t = xs.shape[0]
    tb, nb, fwd, bwd = _scan_specs(t)
    st_shape = (2, SSD_STATE, SSD_INNER)
    return pl.pallas_call(
        _ssd_scan_kernel,
        grid=(nb,),
        in_specs=[fwd(SSD_INNER, 0), fwd(512, 0), fwd(128, 0),
                  bwd(SSD_INNER, 0), bwd(512, 0), bwd(128, 0),
                  _const_spec(dt_bias.shape), _const_spec(a_log.shape), _const_spec(part.shape),
                  _const_spec(cm.shape),
                  _const_spec(expand.shape), _const_spec(diag.shape), _const_spec(causal.shape),
                  _const_spec(st_shape)],
        out_specs=[fwd(SSD_INNER, 0), bwd(SSD_INNER, 0), _const_spec(st_shape)],
        out_shape=[jax.ShapeDtypeStruct((t, SSD_INNER), F32), jax.ShapeDtypeStruct((t, SSD_INNER), F32),
                   jax.ShapeDtypeStruct(st_shape, F32)],
        compiler_params=pltpu.CompilerParams(dimension_semantics=("arbitrary",), vmem_limit_bytes=VMEM_LIMIT),
        name="ssd_scan",
    )(xs, bc, small, xs, bc, small, dt_bias, a_log, part, cm, expand, diag, causal, s0)


def _out_kernel(x_ref, ysf_ref, ysb_ref, xs_ref, z_ref, ygf_ref, ygb_ref, gg_ref, yhf_ref, yhb_ref, hgate_ref,
                dskip_ref, sng_ref, gng_ref, hng_ref, wo_ref, g1_ref, n2g_ref, sh2_ref, sc2_ref, g2_ref,
                w1_ref, w2_ref, fng_ref, o_ref, *, final_norm):
    y_s = ysf_ref[...] + ysb_ref[...] + dskip_ref[...] * xs_ref[...]
    y_s = _rms(y_s * _silu(z_ref[...]), sng_ref[...]).astype(BF16)
    acc = jnp.dot(y_s, wo_ref[0:SSD_INNER, :], preferred_element_type=F32)
    y_g = ygf_ref[...] + ygb_ref[...]
    y_h = yhf_ref[...] + yhb_ref[...]
    gate_g = _silu(gg_ref[...])
    gate_h = _sigmoid(hgate_ref[...])
    for h in range(HEADS):
        sl = slice(h * HEAD_W, (h + 1) * HEAD_W)
        yg = (_rms(y_g[:, sl], gng_ref[...]) * gate_g[:, sl]).astype(BF16)
        yh = (_rms(y_h[:, sl], hng_ref[...]) * gate_h[:, sl]).astype(BF16)
        acc = acc + jnp.dot(yg, wo_ref[SSD_INNER + h * HEAD_W:SSD_INNER + (h + 1) * HEAD_W, :],
                            preferred_element_type=F32)
        acc = acc + jnp.dot(yh, wo_ref[SSD_INNER + MIXW + h * HEAD_W:SSD_INNER + MIXW + (h + 1) * HEAD_W, :],
                            preferred_element_type=F32)
    x1 = x_ref[...] + g1_ref[...] * acc
    hn = (_rms(x1, n2g_ref[...]) * (1.0 + sc2_ref[...]) + sh2_ref[...]).astype(BF16)
    m = jnp.zeros_like(x1)
    fb = 1024
    for j in range(FF // fb):
        a = jnp.maximum(jnp.dot(hn, w1_ref[:, j * fb:(j + 1) * fb], preferred_element_type=F32), 0.0)
        m = m + jnp.dot((a * a).astype(BF16), w2_ref[j * fb:(j + 1) * fb, :], preferred_element_type=F32)
    x2 = x1 + g2_ref[...] * m
    if final_norm:
        x2 = _rms(x2, fng_ref[...])
    o_ref[...] = x2


def _out_block(x2, ysf, ysb, xs, z, ygf, ygb, gg, yhf, yhb, hgate,
               dskip, sng, gng, hng, wo, g1, n2g, sh2, sc2, g2, w1, w2, fng, final_norm):
    t = x2.shape[0]
    tm = min(TM_PROJ, t)
    tile = lambda w: pl.BlockSpec((tm, w), lambda i: (i, 0))
    row = lambda w: pl.BlockSpec((1, w), lambda i: (0, 0))
    resident = lambda shape: pl.BlockSpec(shape, lambda i: (0, 0), pipeline_mode=pl.Buffered(1))
    return pl.pallas_call(
        functools.partial(_out_kernel, final_norm=final_norm),
        grid=(t // tm,),
        in_specs=[tile(D), tile(D), tile(D), tile(D), tile(D),
                  tile(MIXW), tile(MIXW), tile(MIXW), tile(MIXW), tile(MIXW), tile(MIXW),
                  row(D), row(D), row(HEAD_W), row(HEAD_W), resident(wo.shape),
                  row(D), row(D), row(D), row(D), row(D),
                  resident(w1.shape), resident(w2.shape), row(D)],
        out_specs=tile(D),
        out_shape=jax.ShapeDtypeStruct((t, D), F32),
        compiler_params=pltpu.CompilerParams(dimension_semantics=("arbitrary",), vmem_limit_bytes=VMEM_LIMIT),
        name="out_block",
    )(x2, ysf, ysb, xs, z, ygf, ygb, gg, yhf, yhb, hgate,
      dskip, sng, gng, hng, wo, g1, n2g, sh2, sc2, g2, w1, w2, fng)


def _pad_heads(w, heads, width):
    lead = w.shape[:-1]
    w = w.reshape(lead + (heads, width))
    w = jnp.pad(w, [(0, 0)] * len(lead) + [(0, 0), (0, HEAD_W - width)])
    return w.reshape(lead + (heads * HEAD_W,))


def _permute_w_in(w):
    o = _IN_OFFS
    col = lambda i: w[:, o[i]:o[i + 1]]
    dt_f, dt_b = col(2)[:, :SSD_HEADS], col(2)[:, SSD_HEADS:]
    parts = [col(0), col(1),
             _pad_heads(col(3), GLA_HEADS, GLA_HEAD_K), _pad_heads(col(4), GLA_HEADS, GLA_HEAD_K),
             col(5), col(6), col(8), col(9), col(10), col(11)]
    parts += [dt_f] * DT_REP + [dt_b] * DT_REP + [col(7)]
    return jnp.concatenate(parts, axis=1).astype(BF16)


def _dir_lanes(p):
    out = jnp.zeros((2, 1, 128), F32)
    for d in range(2):
        lo = DT_REP * SSD_HEADS * d
        out = out.at[d, 0, lo:lo + DT_REP * SSD_HEADS].set(jnp.tile(p[d], DT_REP))
    return out


def kernel(x, c, ctx, c_ctx, norm1_g, norm2_g, w_mod, b_mod, w_in, ssd_conv_w, ssd_conv_b, ssd_dt_bias,
           ssd_a_log, ssd_d, ssd_norm_g, gla_w_gk2, gla_b_gk, gla_norm_g, hg_lb_logits, hg_norm_g, w_out,
           w_mlp1, w_mlp2, final_norm_g):
    depth = w_in.shape[0]
    assert x.shape[0] == 1 and c.shape[0] == 1 and ctx.shape[0] == 1
    seq, ctx_len = x.shape[1], ctx.shape[1]
    assert seq % TB_SCAN == 0 and ctx_len % CHUNK == 0 and ctx_len <= TB_SCAN

    cm = jnp.asarray(_CM_NP, BF16)
    mask = jnp.asarray(_MASK_NP, F32)
    ssd_cm = jnp.asarray(_SSD_CM_NP, BF16)
    expand = jnp.asarray(_EXPAND_NP, BF16)
    part = jnp.asarray(_PART_NP, jnp.int32)
    diag = jnp.asarray(_DIAG_NP, F32)
    causal = jnp.asarray(_CAUSAL_NP, F32)

    lbs = jnp.cumsum(jax.nn.softmax(hg_lb_logits.astype(F32), axis=0), axis=0)
    lbs = lbs - lbs[0]

    cc = jnp.zeros((8, D), F32).at[0].set(c[0]).at[1].set(c_ctx)
    mod = _modulation(cc, w_mod, b_mod)

    row = lambda v: v.reshape(1, -1)
    xl = x[0]
    xc = ctx[0]
    zeros_g = jnp.zeros((2, HEADS, HEAD_W, HEAD_W), F32)
    zeros_s = jnp.zeros((2, SSD_STATE, SSD_INNER), F32)
    for layer in range(depth):
        last = layer == depth - 1
        w_p = _permute_w_in(w_in[layer])
        wo = w_out[layer].astype(BF16)
        w1 = w_mlp1[layer].astype(BF16)
        w2 = w_mlp2[layer].astype(BF16)
        wg16 = _pad_heads(gla_w_gk2[layer], GLA_HEADS, GLA_HEAD_K).astype(BF16)
        wg = jnp.zeros((2, 128, MIXW), BF16)
        for d in range(2):
            wg = wg.at[d, GLR_LANE + d * GLA_RANK:GLR_LANE + (d + 1) * GLA_RANK].set(wg16[d])
        bg = _pad_heads(gla_b_gk[layer], GLA_HEADS, GLA_HEAD_K).reshape(2, 1, MIXW)
        dtb = _dir_lanes(ssd_dt_bias[layer])
        alog = _dir_lanes(ssd_a_log[layer])
        lb = row(lbs[layer])
        dskip = row(jnp.repeat(ssd_d[layer], SSD_HEADDIM))
        gng, hng = row(gla_norm_g[layer]), row(hg_norm_g[layer])
        sng, n1g, n2g = row(ssd_norm_g[layer]), row(norm1_g[layer]), row(norm2_g[layer])
        fng = row(final_norm_g)
        cw, cb = ssd_conv_w[layer], row(ssd_conv_b[layer])

        def mods(r):
            return [mod[layer, r:r + 1, i * D:(i + 1) * D] for i in range(6)]

        def mix(xin, m, row_len, s_ssd, s_gla, s_hg):
            sh1, sc1 = m[0], m[1]
            z, xs, bc, gqk, gv, gg, hq, hf, hi, hgate, small = _inproj(xin, n1g, sh1, sc1, w_p, cw, cb, row_len)
            ysf, ysb, st_s = _ssd_scan(xs, bc, small, dtb, alog, part, ssd_cm, expand, diag, causal, s_ssd)
            ygf, ygb, st_g = _gla_scan(gqk, gv, small, wg, bg, cm, mask, s_gla)
            yhf, yhb, st_h = _hg_scan(hq, hf, hi, lb, cm, mask, s_hg)
            return (ysf, ysb, xs, z, ygf, ygb, gg, yhf, yhb, hgate), (st_s, st_g, st_h)

        def finish(xin, ys, m, final_norm):
            return _out_block(xin, *ys, dskip, sng, gng, hng, wo, m[2], n2g, m[3], m[4], m[5], w1, w2, fng,
                              final_norm)

        m_ctx, m_lat = mods(1), mods(0)
        ys_ctx, states = mix(xc, m_ctx, ctx_len, zeros_s, zeros_g, zeros_g)
        ys_lat, _ = mix(xl, m_lat, CHUNK, *states)
        xl = finish(xl, ys_lat, m_lat, last)
        if not last:
            xc = finish(xc, ys_ctx, m_ctx, False)
    return xl[None]
```

```python
import functools

import numpy as np
import jax
import jax.numpy as jnp
from jax import lax
from jax.experimental import pallas as pl
from jax.experimental.pallas import tpu as pltpu

F32 = jnp.float32
BF16 = jnp.bfloat16

D = 1024
CHUNK = 64
EPS = 1e-6
SSD_INNER = 1024
SSD_HEADS = 16
SSD_HEADDIM = 64
SSD_STATE = 128
SSD_GROUPS = 2
SSD_CONV = 5
SSD_CONV_DIM = 1536
GLA_HEADS = 4
GLA_HEAD_K = 64
GLA_RANK = 16
GLA_NORMALIZER = 16.0
HEADS = 4
HEAD_W = 128
MIXW = HEADS * HEAD_W
PAIR_W = 2 * HEAD_W
DT_REP = 3
GLR_LANE = 2 * DT_REP * SSD_HEADS
LOG2E = 1.4426950408889634
FF = 4096
N_LEVELS = 6

VMEM_LIMIT = 56 * 1024 * 1024
TM_PROJ = 256
TB_SCAN = 512

_IN_SIZES = (1024, 1536, 32, 256, 256, 512, 512, 32, 512, 1024, 512, 512)
_IN_OFFS = np.concatenate([[0], np.cumsum(_IN_SIZES)]).tolist()

_P_Z, _P_XBC, _P_GQ, _P_GK, _P_GV, _P_GG, _P_HQ, _P_HF, _P_HI, _P_HGATE, _P_SMALL, _P_END = (
    0, 1024, 2560, 3072, 3584, 4096, 4608, 5120, 6144, 6656, 7168, 7296)


def _chunk_constants():
    n = CHUNK
    p = np.arange(n)[:, None]
    r = np.arange(n)[None, :]
    blocks = [(r <= p)]
    masks = [(p == r)]
    for lev in range(1, N_LEVELS + 1):
        size = 1 << lev
        half = size // 2
        start = (p // size) * size
        mid = start + half - 1
        second = (p - start) >= half
        m = np.where(second, (r > mid) & (r <= p), (r > p) & (r <= mid))
        blocks.append(m)
        same = (p // size) == (r // size)
        masks.append(same & second & ((r - (r // size) * size) < half))
    blocks.append(r > p)
    fwd = np.concatenate([b.astype(np.float32) for b in blocks], axis=0)
    fmask = np.stack([m.astype(np.float32) for m in masks], axis=0)
    bwd = np.concatenate([b.astype(np.float32)[::-1, ::-1] for b in blocks], axis=0)
    bmask = fmask[:, ::-1, ::-1]
    cm = np.stack([fwd, bwd], axis=0)
    cm = np.concatenate([cm, cm, cm], axis=2)
    mk = np.stack([fmask, bmask], axis=0)
    mk = np.concatenate([mk, mk], axis=3)
    return cm, np.ascontiguousarray(mk)


def _ssd_constants():
    n = CHUNK
    p = np.arange(n)[:, None]
    r = np.arange(n)[None, :]
    cum_f = (r <= p).astype(np.float32)
    edge_f = (r > p).astype(np.float32)
    cm = np.stack([np.concatenate([cum_f, edge_f], 0),
                   np.concatenate([cum_f[::-1, ::-1], edge_f[::-1, ::-1]], 0)], 0)
    cm = np.concatenate([cm, cm, cm], axis=2)
    lanes = np.arange(SSD_INNER)[None, :]
    head_rows = (np.arange(SSD_HEADS)[:, None] == lanes // SSD_HEADDIM).astype(np.float32)
    expand = np.zeros((2, 128, SSD_INNER), np.float32)
    part = np.full((2, 1, 128), 2, np.int32)
    for d in range(2):
        for rep in range(3):
            lo = DT_REP * SSD_HEADS * d + rep * SSD_HEADS
            expand[d, lo:lo + SSD_HEADS] = head_rows
            part[d, 0, lo:lo + SSD_HEADS] = rep
    s_of_lane = lanes % SSD_HEADDIM
    diag = (p == s_of_lane).astype(np.float32)
    causal = np.stack([(p >= s_of_lane), (p <= s_of_lane)], 0).astype(np.float32)
    return cm, expand, part, diag, causal


_CM_NP, _MASK_NP = _chunk_constants()
_SSD_CM_NP, _EXPAND_NP, _PART_NP, _DIAG_NP, _CAUSAL_NP = _ssd_constants()


def _sigmoid(x):
    return 1.0 / (1.0 + jnp.exp(-x))


def _silu(x):
    return x * _sigmoid(x)


def _log_sigmoid(x):
    return jnp.minimum(x, 0.0) - jnp.log1p(jnp.exp(-jnp.abs(x)))


def _split3(x):
    hi = x.astype(BF16)
    r1 = x - hi.astype(F32)
    mid = r1.astype(BF16)
    r2 = r1 - mid.astype(F32)
    lo = r2.astype(BF16)
    return hi, mid, lo


def _exact_left_mul(m01, x):
    hi, mid, lo = _split3(x)
    return (jnp.dot(m01, hi, preferred_element_type=F32)
            + jnp.dot(m01, mid, preferred_element_type=F32)
            + jnp.dot(m01, lo, preferred_element_type=F32))


def _dot_nt(a, b):
    return lax.dot_general(a, b, (((1,), (1,)), ((), ())), preferred_element_type=F32)


def _dot_tn(a, b):
    return lax.dot_general(a, b, (((0,), (0,)), ((), ())), preferred_element_type=F32)


def _rms(x, gain):
    return x * lax.rsqrt(jnp.mean(x * x, axis=-1, keepdims=True) + EPS) * gain


def _mod_kernel(cc_ref, w_ref, b_ref, o_ref):
    s = _silu(cc_ref[...])
    o_ref[0] = jnp.dot(s, w_ref[0], precision=lax.Precision.HIGHEST, preferred_element_type=F32) + b_ref[0]


def _modulation(cc, w_mod, b_mod):
    depth, _, n_mod = w_mod.shape
    bn = 1536
    return pl.pallas_call(
        _mod_kernel,
        grid=(depth, n_mod // bn),
        in_specs=[pl.BlockSpec((8, D), lambda l, j: (0, 0)),
                  pl.BlockSpec((1, D, bn), lambda l, j: (l, 0, j)),
                  pl.BlockSpec((1, 1, bn), lambda l, j: (l, 0, j))],
        out_specs=pl.BlockSpec((1, 8, bn), lambda l, j: (l, 0, j)),
        out_shape=jax.ShapeDtypeStruct((depth, 8, n_mod), F32),
        compiler_params=pltpu.CompilerParams(dimension_semantics=("arbitrary", "arbitrary"),
                                             vmem_limit_bytes=VMEM_LIMIT),
        name="modulation",
    )(cc, w_mod, b_mod.reshape(depth, 1, n_mod))


def _inproj_kernel(x_ref, g_ref, sh_ref, sc_ref, w_ref, cw_ref, cb_ref,
                   z_ref, xs_ref, bc_ref, gqk_ref, gv_ref, gg_ref, hq_ref, hf_ref, hi_ref, hgate_ref,
                   small_ref, *, row_len):
    x = x_ref[...]
    tm = x.shape[0]
    h = _rms(x, g_ref[...]) * (1.0 + sc_ref[...]) + sh_ref[...]
    hb = h.astype(BF16)

    def proj(a, b):
        return jnp.dot(hb, w_ref[:, a:b], preferred_element_type=F32)

    z_ref[...] = proj(_P_Z, _P_XBC)

    pos = lax.broadcasted_iota(jnp.int32, (tm, 128), 0) & (row_len - 1)
    shifts = [k - SSD_CONV // 2 for k in range(SSD_CONV)]
    valid = {s: ((pos + s >= 0) & (pos + s < row_len)) for s in shifts if s != 0}

    def conv_block(uj, lo):
        acc = uj * cw_ref[SSD_CONV // 2:SSD_CONV // 2 + 1, lo:lo + 128] + cb_ref[:, lo:lo + 128]
        for k, s in enumerate(shifts):
            if s == 0:
                continue
            us = pltpu.roll(uj, (-s) % tm, axis=0)
            acc = acc + jnp.where(valid[s], us, 0.0) * cw_ref[k:k + 1, lo:lo + 128]
        v = _silu(acc)
        if lo < SSD_INNER:
            xs_ref[:, lo:lo + 128] = v
        else:
            bc_ref[:, lo - SSD_INNER:lo - SSD_INNER + 128] = v.astype(BF16)

    def p_gqk():
        gqk_ref[:, 0:MIXW] = proj(_P_GQ, _P_GK) * (GLA_HEAD_K ** -0.5)
        gqk_ref[:, MIXW:2 * MIXW] = proj(_P_GK, _P_GV)

    def p_gv_gg():
        gv_ref[...] = proj(_P_GV, _P_GG).astype(BF16)
        gg_ref[...] = proj(_P_GG, _P_HQ)

    def p_hq():
        hq_ref[...] = _silu(proj(_P_HQ, _P_HF))

    def p_hf():
        hf_ref[...] = proj(_P_HF, _P_HI)

    def p_hi_hgate():
        hi_ref[...] = proj(_P_HI, _P_HGATE).astype(BF16)
        hgate_ref[...] = proj(_P_HGATE, _P_SMALL)

    def p_small():
        small_ref[...] = proj(_P_SMALL, _P_END)

    others = [p_gqk, p_gv_gg, p_hq, p_hf, p_hi_hgate, p_small]
    cw = 256
    for j in range(SSD_CONV_DIM // cw):
        u = proj(_P_XBC + j * cw, _P_XBC + (j + 1) * cw)
        others[j]()
        for i in range(cw // 128):
            conv_block(u[:, i * 128:(i + 1) * 128], j * cw + i * 128)


def _inproj(x2, gain, shift, scale, w_p, conv_w, conv_b, row_len):
    t = x2.shape[0]
    tm = min(TM_PROJ, t)
    row = lambda w: pl.BlockSpec((1, w), lambda i: (0, 0))
    tile = lambda w: pl.BlockSpec((tm, w), lambda i: (i, 0))
    outs = [(D, F32), (SSD_INNER, F32), (512, BF16), (2 * MIXW, F32), (MIXW, BF16), (MIXW, F32),
            (MIXW, F32), (2 * MIXW, F32), (MIXW, BF16), (MIXW, F32), (128, F32)]
    return pl.pallas_call(
        functools.partial(_inproj_kernel, row_len=row_len),
        grid=(t // tm,),
        in_specs=[tile(D), row(D), row(D), row(D),
                  pl.BlockSpec((D, _P_END), lambda i: (0, 0), pipeline_mode=pl.Buffered(1)),
                  pl.BlockSpec((SSD_CONV, SSD_CONV_DIM), lambda i: (0, 0)),
                  row(SSD_CONV_DIM)],
        out_specs=[tile(w) for w, _ in outs],
        out_shape=[jax.ShapeDtypeStruct((t, w), dt) for w, dt in outs],
        compiler_params=pltpu.CompilerParams(dimension_semantics=("arbitrary",), vmem_limit_bytes=VMEM_LIMIT),
        name="inproj",
    )(x2, gain, shift, scale, w_p, conv_w, conv_b)


def _pair_block_diag(a):
    z = jnp.zeros((a.shape[0], HEAD_W), a.dtype)
    return jnp.concatenate([jnp.concatenate([a[:, :HEAD_W], z], axis=1),
                            jnp.concatenate([z, a[:, HEAD_W:]], axis=1)], axis=0)


def _gla_chunks(streams, st_ref, cm_ref, mask_ref):
    def blk(x, i):
        return x[i * CHUNK:(i + 1) * CHUNK]

    xs, ks = [], []
    for (_, _, d, prep) in streams:
        k, g = prep()
        ks.append(k)
        g2 = g * LOG2E
        e = jnp.dot(cm_ref[d], jnp.concatenate(_split3(g2), axis=0), preferred_element_type=F32)
        xs.append(jnp.exp2(e))

    atts = []
    for (q, _, d, _), k, x in zip(streams, ks, xs):
        qb, kb = q.astype(BF16), k.astype(BF16)
        for p in range(HEADS // 2):
            sl = slice(p * PAIR_W, (p + 1) * PAIR_W)
            qp, kp = qb[:, sl], kb[:, sl]
            att = mask_ref[d, 0] * _dot_nt(qp, _pair_block_diag(kp))
            for lev in range(1, N_LEVELS + 1):
                xl = blk(x, lev)[:, sl].astype(BF16)
                att = att + mask_ref[d, lev] * _dot_nt(qp * xl, _pair_block_diag(kp * xl))
            atts.append(att)

    outs = []
    zs = jnp.zeros((HEAD_W, HEAD_W), BF16)
    for si, ((q, v, d, _), k, x) in enumerate(zip(streams, ks, xs)):
        edge_row = (CHUNK - 1) if d == 0 else 0
        xb = blk(x, 0)
        xe = blk(x, N_LEVELS + 1)
        parts = []
        for p in range(HEADS // 2):
            sl = slice(p * PAIR_W, (p + 1) * PAIR_W)
            vp = v[:, sl]
            st0, st1 = st_ref[d, 2 * p], st_ref[d, 2 * p + 1]
            st_bd = jnp.concatenate([jnp.concatenate([st0.astype(BF16), zs], axis=1),
                                     jnp.concatenate([zs, st1.astype(BF16)], axis=1)], axis=0)
            o = jnp.dot(atts[si * (HEADS // 2) + p].astype(BF16), _pair_block_diag(vp),
                        preferred_element_type=F32)
            o = o + _dot_nt((q[:, sl] * xb[:, sl]).astype(BF16), st_bd)
            kx = (k[:, sl] * xe[:, sl]).astype(BF16)
            decay = xb[edge_row:edge_row + 1, sl]
            for i, st in enumerate((st0, st1)):
                hs = slice(i * HEAD_W, (i + 1) * HEAD_W)
                st_ref[d, 2 * p + i] = st * decay[:, hs] + _dot_tn(vp[:, hs], kx[:, hs])
            parts.append(o)
        outs.append(jnp.concatenate(parts, axis=1))
    return outs


def _scan_loop(n_chunks, body):
    def step(c, carry):
        body(c)
        return carry
    lax.fori_loop(0, n_chunks, step, 0)


def _gla_scan_kernel(qf_ref, kf_ref, vf_ref, sf_ref, qb_ref, kb_ref, vb_ref, sb_ref,
                     wg_ref, bg_ref, cm_ref, mask_ref, s0_ref,
                     of_ref, ob_ref, st_ref):
    @pl.when(pl.program_id(0) == 0)
    def _():
        st_ref[...] = s0_ref[...]

    n_chunks = qf_ref.shape[0] // CHUNK

    def gate(small, d):
        logit = jnp.dot(small.astype(BF16), wg_ref[d], preferred_element_type=F32) + bg_ref[d]
        return _log_sigmoid(logit) / GLA_NORMALIZER

    def body(c):
        rf = pl.ds(pl.multiple_of(c * CHUNK, CHUNK), CHUNK)
        rb = pl.ds(pl.multiple_of((n_chunks - 1 - c) * CHUNK, CHUNK), CHUNK)
        kg_f = (kf_ref[rf, :], gate(sf_ref[rf, :], 0))
        kg_b = (kb_ref[rb, :], gate(sb_ref[rb, :], 1))
        streams = [(qf_ref[rf, :], vf_ref[rf, :], 0, lambda: kg_f), (qb_ref[rb, :], vb_ref[rb, :], 1, lambda: kg_b)]
        of_ref[rf, :], ob_ref[rb, :] = _gla_chunks(streams, st_ref, cm_ref, mask_ref)

    _scan_loop(n_chunks, body)


def _hg_scan_kernel(qf_ref, ff_ref, vf_ref, qb_ref, fb_ref, vb_ref,
                    lb_ref, cm_ref, mask_ref, s0_ref,
                    of_ref, ob_ref, st_ref):
    @pl.when(pl.program_id(0) == 0)
    def _():
        st_ref[...] = s0_ref[...]

    n_chunks = qf_ref.shape[0] // CHUNK
    lb = lb_ref[...]
    log_lb = jnp.log(lb)
    log1m_lb = jnp.log1p(-lb)
    one_m_lb = 1.0 - lb

    def gates(hf):
        e = jnp.exp(-jnp.abs(hf))
        log_sig = jnp.minimum(hf, 0.0) - jnp.log1p(e)
        t = log1m_lb + log_sig
        m = jnp.maximum(log_lb, t)
        log_f = m + jnp.log1p(jnp.exp(-jnp.abs(log_lb - t)))
        k = one_m_lb * (jnp.where(hf >= 0.0, e, 1.0) / (1.0 + e))
        return k, log_f

    def body(c):
        rf = pl.ds(pl.multiple_of(c * CHUNK, CHUNK), CHUNK)
        rb = pl.ds(pl.multiple_of((n_chunks - 1 - c) * CHUNK, CHUNK), CHUNK)
        streams = [(qf_ref[rf, :], vf_ref[rf, :], 0, lambda: gates(ff_ref[rf, :])),
                   (qb_ref[rb, :], vb_ref[rb, :], 1, lambda: gates(fb_ref[rb, :]))]
        of_ref[rf, :], ob_ref[rb, :] = _gla_chunks(streams, st_ref, cm_ref, mask_ref)

    _scan_loop(n_chunks, body)


def _const_spec(shape):
    nd = len(shape)
    return pl.BlockSpec(shape, lambda i: (0,) * nd)


def _scan_specs(t):
    tb = min(TB_SCAN, t)
    nb = t // tb
    fwd = lambda w, j: pl.BlockSpec((tb, w), lambda i: (i, j))
    bwd = lambda w, j: pl.BlockSpec((tb, w), lambda i: (nb - 1 - i, j))
    return tb, nb, fwd, bwd


def _gla_scan(gqk, gv, small, wg, bg, cm, mask, s0):
    t = gqk.shape[0]
    tb, nb, fwd, bwd = _scan_specs(t)
    st_shape = (2, HEADS, HEAD_W, HEAD_W)
    return pl.pallas_call(
        _gla_scan_kernel,
        grid=(nb,),
        in_specs=[fwd(MIXW, 0), fwd(MIXW, 1), fwd(MIXW, 0), fwd(128, 0),
                  bwd(MIXW, 0), bwd(MIXW, 1), bwd(MIXW, 0), bwd(128, 0),
                  _const_spec(wg.shape), _const_spec(bg.shape), _const_spec(cm.shape), _const_spec(mask.shape),
                  _const_spec(st_shape)],
        out_specs=[fwd(MIXW, 0), bwd(MIXW, 0), _const_spec(st_shape)],
        out_shape=[jax.ShapeDtypeStruct((t, MIXW), F32), jax.ShapeDtypeStruct((t, MIXW), F32),
                   jax.ShapeDtypeStruct(st_shape, F32)],
        compiler_params=pltpu.CompilerParams(dimension_semantics=("arbitrary",), vmem_limit_bytes=VMEM_LIMIT),
        name="gla_scan",
    )(gqk, gqk, gv, small, gqk, gqk, gv, small, wg, bg, cm, mask, s0)


def _hg_scan(hq, hf, hi, lb, cm, mask, s0):
    t = hq.shape[0]
    tb, nb, fwd, bwd = _scan_specs(t)
    st_shape = (2, HEADS, HEAD_W, HEAD_W)
    return pl.pallas_call(
        _hg_scan_kernel,
        grid=(nb,),
        in_specs=[fwd(MIXW, 0), fwd(MIXW, 0), fwd(MIXW, 0),
                  bwd(MIXW, 0), bwd(MIXW, 1), bwd(MIXW, 0),
                  _const_spec(lb.shape), _const_spec(cm.shape), _const_spec(mask.shape), _const_spec(st_shape)],
        out_specs=[fwd(MIXW, 0), bwd(MIXW, 0), _const_spec(st_shape)],
        out_shape=[jax.ShapeDtypeStruct((t, MIXW), F32), jax.ShapeDtypeStruct((t, MIXW), F32),
                   jax.ShapeDtypeStruct(st_shape, F32)],
        compiler_params=pltpu.CompilerParams(dimension_semantics=("arbitrary",), vmem_limit_bytes=VMEM_LIMIT),
        name="hg_scan",
    )(hq, hf, hi, hq, hf, hi, lb, cm, mask, s0)


def _ssd_chunks(streams, st_ref, dtb_ref, alog_ref, part_ref, cm_ref, ex_ref, diag_ref, causal_ref):
    gw = SSD_INNER // SSD_GROUPS
    lane = lax.broadcasted_iota(jnp.int32, (CHUNK, 128), 1)
    first = lane < SSD_HEADDIM
    zx = jnp.zeros((128, 128), BF16)

    stage = []
    for (xs, _, small, d) in streams:
        dt = small + dtb_ref[d]
        dt = jnp.maximum(dt, 0.0) + jnp.log1p(jnp.exp(-jnp.abs(dt)))
        a = dt * (-jnp.exp(alog_ref[d]))
        ce = jnp.dot(cm_ref[d], jnp.concatenate(_split3(a), axis=0), preferred_element_type=F32)
        acs, aed = ce[0:CHUNK], ce[CHUNK:2 * CHUNK]
        sm = jnp.concatenate([dt, acs, jnp.exp(acs), jnp.exp(aed)], axis=0)
        r1 = sm - sm.astype(BF16).astype(F32)
        r2 = r1 - r1.astype(BF16).astype(F32)
        part = part_ref[d]
        parts = jnp.where(part == 0, sm, jnp.where(part == 1, r1, r2)).astype(BF16)
        big = jnp.dot(parts, ex_ref[d], preferred_element_type=F32)
        dt_x, acs_x = big[0:CHUNK], big[CHUNK:2 * CHUNK]
        dec_x, edge_x = big[2 * CHUNK:3 * CHUNK], big[3 * CHUNK:4 * CHUNK]
        acs_row = jnp.sum(acs_x * diag_ref[...], axis=0, keepdims=True)
        lmat = jnp.exp(jnp.where(causal_ref[d] > 0.0, acs_x - acs_row, -jnp.inf))
        xdt = xs * dt_x
        xed = (xdt * edge_x).astype(BF16)
        stage.append((lmat, xdt, xed, dec_x))

    outs = []
    for (_, bc, _, d), (lmat, xdt, xed, dec_x) in zip(streams, stage):
        edge_row = (CHUNK - 1) if d == 0 else 0
        parts = []
        for grp in range(SSD_GROUPS):
            gl = slice(grp * gw, (grp + 1) * gw)
            b_g = bc[:, grp * SSD_STATE:(grp + 1) * SSD_STATE]
            c_g = bc[:, (SSD_GROUPS + grp) * SSD_STATE:(SSD_GROUPS + grp + 1) * SSD_STATE]
            scores2 = _dot_nt(c_g, jnp.concatenate([b_g, b_g], axis=0))
            scores4 = jnp.concatenate([scores2, scores2], axis=1)
            st = st_ref[d, :, gl]
            y_off = jnp.dot(c_g, st.astype(BF16), preferred_element_type=F32) * dec_x[:, gl]
            for j in range(gw // 256):
                ll = slice(grp * gw + j * 256, grp * gw + (j + 1) * 256)
                w = (scores4 * lmat[:, ll]).astype(BF16)
                xj = xdt[:, ll]
                xm = []
                for i in range(2):
                    xi = xj[:, i * 128:(i + 1) * 128]
                    xm.append(jnp.concatenate([jnp.where(first, xi, 0.0), jnp.where(first, 0.0, xi)],
                                              axis=0).astype(BF16))
                xbd = jnp.concatenate([jnp.concatenate([xm[0], zx], axis=1),
                                       jnp.concatenate([zx, xm[1]], axis=1)], axis=0)
                parts.append(jnp.dot(w, xbd, preferred_element_type=F32) + y_off[:, j * 256:(j + 1) * 256])
            st_ref[d, :, gl] = st * dec_x[edge_row:edge_row + 1, gl] + _dot_tn(b_g, xed[:, gl])
        outs.append(jnp.concatenate(parts, axis=1))
    return outs


def _ssd_scan_kernel(xf_ref, bcf_ref, sf_ref, xb_ref, bcb_ref, sb_ref,
                     dtb_ref, alog_ref, part_ref, cm_ref, ex_ref, diag_ref, causal_ref, s0_ref,
                     of_ref, ob_ref, st_ref):
    @pl.when(pl.program_id(0) == 0)
    def _():
        st_ref[...] = s0_ref[...]

    n_chunks = xf_ref.shape[0] // CHUNK

    def body(c):
        rf = pl.ds(pl.multiple_of(c * CHUNK, CHUNK), CHUNK)
        rb = pl.ds(pl.multiple_of((n_chunks - 1 - c) * CHUNK, CHUNK), CHUNK)
        streams = [(xf_ref[rf, :], bcf_ref[rf, :], sf_ref[rf, :], 0),
                   (xb_ref[rb, :], bcb_ref[rb, :], sb_ref[rb, :], 1)]
        of_ref[rf, :], ob_ref[rb, :] = _ssd_chunks(streams, st_ref, dtb_ref, alog_ref, part_ref, cm_ref,
                                                   ex_ref, diag_ref, causal_ref)

    _scan_loop(n_chunks, body)


def _ssd_scan(xs, bc, small, dt_bias, a_log, part, cm, expand, diag, causal, s0):
    t = xs.shape[0]
    tb, nb, fwd, bwd = _scan_specs(t)
    st_shape = (2, SSD_STATE, SSD_INNER)
    return pl.pallas_call(
        _ssd_scan_kernel,
        grid=(nb,),
        in_specs=[fwd(SSD_INNER, 0), fwd(512, 0), fwd(128, 0),
                  bwd(SSD_INNER, 0), bwd(512, 0), bwd(128, 0),
                  _const_spec(dt_bias.shape), _const_spec(a_log.shape), _const_spec(part.shape),
                  _const_spec(cm.shape),
                  _const_spec(expand.shape), _const_spec(diag.shape), _const_spec(causal.shape),
                  _const_spec(st_shape)],
        out_specs=[fwd(SSD_INNER, 0), bwd(SSD_INNER, 0), _const_spec(st_shape)],
        out_shape=[jax.ShapeDtypeStruct((t, SSD_INNER), F32), jax.ShapeDtypeStruct((t, SSD_INNER), F32),
                   jax.ShapeDtypeStruct(st_shape, F32)],
        compiler_params=pltpu.CompilerParams(dimension_semantics=("arbitrary",), vmem_limit_bytes=VMEM_LIMIT),
        name="ssd_scan",
    )(xs, bc, small, xs, bc, small, dt_bias, a_log, part, cm, expand, diag, causal, s0)


def _out_kernel(x_ref, ysf_ref, ysb_ref, xs_ref, z_ref, ygf_ref, ygb_ref, gg_ref, yhf_ref, yhb_ref, hgate_ref,
                dskip_ref, sng_ref, gng_ref, hng_ref, wo_ref, g1_ref, n2g_ref, sh2_ref, sc2_ref, g2_ref,
                w1_ref, w2_ref, fng_ref, o_ref, *, final_norm):
    y_s = ysf_ref[...] + ysb_ref[...] + dskip_ref[...] * xs_ref[...]
    y_s = _rms(y_s * _silu(z_ref[...]), sng_ref[...]).astype(BF16)
    acc = jnp.dot(y_s, wo_ref[0:SSD_INNER, :], preferred_element_type=F32)
    y_g = ygf_ref[...] + ygb_ref[...]
    y_h = yhf_ref[...] + yhb_ref[...]
    gate_g = _silu(gg_ref[...])
    gate_h = _sigmoid(hgate_ref[...])
    for h in range(HEADS):
        sl = slice(h * HEAD_W, (h + 1) * HEAD_W)
        yg = (_rms(y_g[:, sl], gng_ref[...]) * gate_g[:, sl]).astype(BF16)
        yh = (_rms(y_h[:, sl], hng_ref[...]) * gate_h[:, sl]).astype(BF16)
        acc = acc + jnp.dot(yg, wo_ref[SSD_INNER + h * HEAD_W:SSD_INNER + (h + 1) * HEAD_W, :],
                            preferred_element_type=F32)
        acc = acc + jnp.dot(yh, wo_ref[SSD_INNER + MIXW + h * HEAD_W:SSD_INNER + MIXW + (h + 1) * HEAD_W, :],
                            preferred_element_type=F32)
    x1 = x_ref[...] + g1_ref[...] * acc
    hn = (_rms(x1, n2g_ref[...]) * (1.0 + sc2_ref[...]) + sh2_ref[...]).astype(BF16)
    m = jnp.zeros_like(x1)
    fb = 1024
    for j in range(FF // fb):
        a = jnp.maximum(jnp.dot(hn, w1_ref[:, j * fb:(j + 1) * fb], preferred_element_type=F32), 0.0)
        m = m + jnp.dot((a * a).astype(BF16), w2_ref[j * fb:(j + 1) * fb, :], preferred_element_type=F32)
    x2 = x1 + g2_ref[...] * m
    if final_norm:
        x2 = _rms(x2, fng_ref[...])
    o_ref[...] = x2


def _out_block(x2, ysf, ysb, xs, z, ygf, ygb, gg, yhf, yhb, hgate,
               dskip, sng, gng, hng, wo, g1, n2g, sh2, sc2, g2, w1, w2, fng, final_norm):
    t = x2.shape[0]
    tm = min(TM_PROJ, t)
    tile = lambda w: pl.BlockSpec((tm, w), lambda i: (i, 0))
    row = lambda w: pl.BlockSpec((1, w), lambda i: (0, 0))
    resident = lambda shape: pl.BlockSpec(shape, lambda i: (0, 0), pipeline_mode=pl.Buffered(1))
    return pl.pallas_call(
        functools.partial(_out_kernel, final_norm=final_norm),
        grid=(t // tm,),
        in_specs=[tile(D), tile(D), tile(D), tile(D), tile(D),
                  tile(MIXW), tile(MIXW), tile(MIXW), tile(MIXW), tile(MIXW), tile(MIXW),
                  row(D), row(D), row(HEAD_W), row(HEAD_W), resident(wo.shape),
                  row(D), row(D), row(D), row(D), row(D),
                  resident(w1.shape), resident(w2.shape), row(D)],
        out_specs=tile(D),
        out_shape=jax.ShapeDtypeStruct((t, D), F32),
        compiler_params=pltpu.CompilerParams(dimension_semantics=("arbitrary",), vmem_limit_bytes=VMEM_LIMIT),
        name="out_block",
    )(x2, ysf, ysb, xs, z, ygf, ygb, gg, yhf, yhb, hgate,
      dskip, sng, gng, hng, wo, g1, n2g, sh2, sc2, g2, w1, w2, fng)


def _pad_heads(w, heads, width):
    lead = w.shape[:-1]
    w = w.reshape(lead + (heads, width))
    w = jnp.pad(w, [(0, 0)] * len(lead) + [(0, 0), (0, HEAD_W - width)])
    return w.reshape(lead + (heads * HEAD_W,))


def _permute_w_in(w):
    o = _IN_OFFS
    col = lambda i: w[:, o[i]:o[i + 1]]
    dt_f, dt_b = col(2)[:, :SSD_HEADS], col(2)[:, SSD_HEADS:]
    parts = [col(0), col(1),
             _pad_heads(col(3), GLA_HEADS, GLA_HEAD_K), _pad_heads(col(4), GLA_HEADS, GLA_HEAD_K),
             col(5), col(6), col(8), col(9), col(10), col(11)]
    parts += [dt_f] * DT_REP + [dt_b] * DT_REP + [col(7)]
    return jnp.concatenate(parts, axis=1).astype(BF16)


def _dir_lanes(p):
    out = jnp.zeros((2, 1, 128), F32)
    for d in range(2):
        lo = DT_REP * SSD_HEADS * d
        out = out.at[d, 0, lo:lo + DT_REP * SSD_HEADS].set(jnp.tile(p[d], DT_REP))
    return out


def kernel(x, c, ctx, c_ctx, norm1_g, norm2_g, w_mod, b_mod, w_in, ssd_conv_w, ssd_conv_b, ssd_dt_bias,
           ssd_a_log, ssd_d, ssd_norm_g, gla_w_gk2, gla_b_gk, gla_norm_g, hg_lb_logits, hg_norm_g, w_out,
           w_mlp1, w_mlp2, final_norm_g):
    depth = w_in.shape[0]
    assert x.shape[0] == 1 and c.shape[0] == 1 and ctx.shape[0] == 1
    seq, ctx_len = x.shape[1], ctx.shape[1]
    assert seq % TB_SCAN == 0 and ctx_len % CHUNK == 0 and ctx_len <= TB_SCAN

    cm = jnp.asarray(_CM_NP, BF16)
    mask = jnp.asarray(_MASK_NP, F32)
    ssd_cm = jnp.asarray(_SSD_CM_NP, BF16)
    expand = jnp.asarray(_EXPAND_NP, BF16)
    part = jnp.asarray(_PART_NP, jnp.int32)
    diag = jnp.asarray(_DIAG_NP, F32)
    causal = jnp.asarray(_CAUSAL_NP, F32)

    lbs = jnp.cumsum(jax.nn.softmax(hg_lb_logits.astype(F32), axis=0), axis=0)
    lbs = lbs - lbs[0]

    cc = jnp.zeros((8, D), F32).at[0].set(c[0]).at[1].set(c_ctx)
    mod = _modulation(cc, w_mod, b_mod)

    row = lambda v: v.reshape(1, -1)
    xl = x[0]
    xc = ctx[0]
    zeros_g = jnp.zeros((2, HEADS, HEAD_W, HEAD_W), F32)
    zeros_s = jnp.zeros((2, SSD_STATE, SSD_INNER), F32)
    for layer in range(depth):
        last = layer == depth - 1
        w_p = _permute_w_in(w_in[layer])
        wo = w_out[layer].astype(BF16)
        w1 = w_mlp1[layer].astype(BF16)
        w2 = w_mlp2[layer].astype(BF16)
        wg16 = _pad_heads(gla_w_gk2[layer], GLA_HEADS, GLA_HEAD_K).astype(BF16)
        wg = jnp.zeros((2, 128, MIXW), BF16)
        for d in range(2):
            wg = wg.at[d, GLR_LANE + d * GLA_RANK:GLR_LANE + (d + 1) * GLA_RANK].set(wg16[d])
        bg = _pad_heads(gla_b_gk[layer], GLA_HEADS, GLA_HEAD_K).reshape(2, 1, MIXW)
        dtb = _dir_lanes(ssd_dt_bias[layer])
        alog = _dir_lanes(ssd_a_log[layer])
        lb = row(lbs[layer])
        dskip = row(jnp.repeat(ssd_d[layer], SSD_HEADDIM))
        gng, hng = row(gla_norm_g[layer]), row(hg_norm_g[layer])
        sng, n1g, n2g = row(ssd_norm_g[layer]), row(norm1_g[layer]), row(norm2_g[layer])
        fng = row(final_norm_g)
        cw, cb = ssd_conv_w[layer], row(ssd_conv_b[layer])

        def mods(r):
            return [mod[layer, r:r + 1, i * D:(i + 1) * D] for i in range(6)]

        def mix(xin, m, row_len, s_ssd, s_gla, s_hg):
            sh1, sc1 = m[0], m[1]
            z, xs, bc, gqk, gv, gg, hq, hf, hi, hgate, small = _inproj(xin, n1g, sh1, sc1, w_p, cw, cb, row_len)
            ysf, ysb, st_s = _ssd_scan(xs, bc, small, dtb, alog, part, ssd_cm, expand, diag, causal, s_ssd)
            ygf, ygb, st_g = _gla_scan(gqk, gv, small, wg, bg, cm, mask, s_gla)
            yhf, yhb, st_h = _hg_scan(hq, hf, hi, lb, cm, mask, s_hg)
            return (ysf, ysb, xs, z, ygf, ygb, gg, yhf, yhb, hgate), (st_s, st_g, st_h)

        def finish(xin, ys, m, final_norm):
            return _out_block(xin, *ys, dskip, sng, gng, hng, wo, m[2], n2g, m[3], m[4], m[5], w1, w2, fng,
                              final_norm)

        m_ctx, m_lat = mods(1), mods(0)
        ys_ctx, states = mix(xc, m_ctx, ctx_len, zeros_s, zeros_g, zeros_g)
        ys_lat, _ = mix(xl, m_lat, CHUNK, *states)
        xl = finish(xl, ys_lat, m_lat, last)
        if not last:
            xc = finish(xc, ys_ctx, m_ctx, False)
    return xl[None]
```

```python
import functools

import numpy as np
import jax
import jax.numpy as jnp
from jax import lax
from jax.experimental import pallas as pl
from jax.experimental.pallas import tpu as pltpu

F32 = jnp.float32
BF16 = jnp.bfloat16

D = 1024
CHUNK = 64
EPS = 1e-6
SSD_INNER = 1024
SSD_HEADS = 16
SSD_HEADDIM = 64
SSD_STATE = 128
SSD_GROUPS = 2
SSD_CONV = 5
SSD_CONV_DIM = 1536
GLA_HEADS = 4
GLA_HEAD_K = 64
GLA_RANK = 16
GLA_NORMALIZER = 16.0
HEADS = 4
HEAD_W = 128
MIXW = HEADS * HEAD_W
PAIR_W = 2 * HEAD_W
DT_REP = 3
GLR_LANE = 2 * DT_REP * SSD_HEADS
LOG2E = 1.4426950408889634
FF = 4096
N_LEVELS = 6

VMEM_LIMIT = 56 * 1024 * 1024
TM_PROJ = 256
TB_SCAN = 512
SCAN_UNROLL = 4

_IN_SIZES = (1024, 1536, 32, 256, 256, 512, 512, 32, 512, 1024, 512, 512)
_IN_OFFS = np.concatenate([[0], np.cumsum(_IN_SIZES)]).tolist()

_P_Z, _P_XBC, _P_GQ, _P_GK, _P_GV, _P_GG, _P_HQ, _P_HF, _P_HI, _P_HGATE, _P_SMALL, _P_END = (
    0, 1024, 2560, 3072, 3584, 4096, 4608, 5120, 6144, 6656, 7168, 7296)


def _chunk_constants():
    n = CHUNK
    p = np.arange(n)[:, None]
    r = np.arange(n)[None, :]
    blocks = [(r <= p)]
    masks = [(p == r)]
    for lev in range(1, N_LEVELS + 1):
        size = 1 << lev
        half = size // 2
        start = (p // size) * size
        mid = start + half - 1
        second = (p - start) >= half
        m = np.where(second, (r > mid) & (r <= p), (r > p) & (r <= mid))
        blocks.append(m)
        same = (p // size) == (r // size)
        masks.append(same & second & ((r - (r // size) * size) < half))
    blocks.append(r > p)
    fwd = np.concatenate([b.astype(np.float32) for b in blocks], axis=0)
    fmask = np.stack([m.astype(np.float32) for m in masks], axis=0)
    bwd = np.concatenate([b.astype(np.float32)[::-1, ::-1] for b in blocks], axis=0)
    bmask = fmask[:, ::-1, ::-1]
    cm = np.stack([fwd, bwd], axis=0)
    cm = np.concatenate([cm, cm, cm], axis=2)
    mk = np.stack([fmask, bmask], axis=0)
    mk = np.concatenate([mk, mk], axis=3)
    return cm, np.ascontiguousarray(mk)


def _ssd_constants():
    n = CHUNK
    p = np.arange(n)[:, None]
    r = np.arange(n)[None, :]
    cum_f = (r <= p).astype(np.float32)
    edge_f = (r > p).astype(np.float32)
    cm = np.stack([np.concatenate([cum_f, edge_f], 0),
                   np.concatenate([cum_f[::-1, ::-1], edge_f[::-1, ::-1]], 0)], 0)
    cm = np.concatenate([cm, cm, cm], axis=2)
    lanes = np.arange(SSD_INNER)[None, :]
    head_rows = (np.arange(SSD_HEADS)[:, None] == lanes // SSD_HEADDIM).astype(np.float32)
    expand = np.zeros((2, 128, SSD_INNER), np.float32)
    part = np.full((2, 1, 128), 2, np.int32)
    for d in range(2):
        for rep in range(3):
            lo = DT_REP * SSD_HEADS * d + rep * SSD_HEADS
            expand[d, lo:lo + SSD_HEADS] = head_rows
            part[d, 0, lo:lo + SSD_HEADS] = rep
    s_of_lane = lanes % SSD_HEADDIM
    diag = (p == s_of_lane).astype(np.float32)
    causal = np.stack([(p >= s_of_lane), (p <= s_of_lane)], 0).astype(np.float32)
    return cm, expand, part, diag, causal


_CM_NP, _MASK_NP = _chunk_constants()
_SSD_CM_NP, _EXPAND_NP, _PART_NP, _DIAG_NP, _CAUSAL_NP = _ssd_constants()


def _sigmoid(x):
    return 1.0 / (1.0 + jnp.exp(-x))


def _silu(x):
    return x * _sigmoid(x)


def _log_sigmoid(x):
    return jnp.minimum(x, 0.0) - jnp.log1p(jnp.exp(-jnp.abs(x)))


def _split3(x):
    hi = x.astype(BF16)
    r1 = x - hi.astype(F32)
    mid = r1.astype(BF16)
    r2 = r1 - mid.astype(F32)
    lo = r2.astype(BF16)
    return hi, mid, lo


def _exact_left_mul(m01, x):
    hi, mid, lo = _split3(x)
    return (jnp.dot(m01, hi, preferred_element_type=F32)
            + jnp.dot(m01, mid, preferred_element_type=F32)
            + jnp.dot(m01, lo, preferred_element_type=F32))


def _dot_nt(a, b):
    return lax.dot_general(a, b, (((1,), (1,)), ((), ())), preferred_element_type=F32)


def _dot_tn(a, b):
    return lax.dot_general(a, b, (((0,), (0,)), ((), ())), preferred_element_type=F32)


def _rms(x, gain):
    return x * lax.rsqrt(jnp.mean(x * x, axis=-1, keepdims=True) + EPS) * gain


def _mod_kernel(cc_ref, w_ref, b_ref, o_ref):
    s = _silu(cc_ref[...])
    o_ref[0] = jnp.dot(s, w_ref[0], precision=lax.Precision.HIGHEST, preferred_element_type=F32) + b_ref[0]


def _modulation(cc, w_mod, b_mod):
    depth, _, n_mod = w_mod.shape
    bn = 1536
    return pl.pallas_call(
        _mod_kernel,
        grid=(depth, n_mod // bn),
        in_specs=[pl.BlockSpec((8, D), lambda l, j: (0, 0)),
                  pl.BlockSpec((1, D, bn), lambda l, j: (l, 0, j)),
                  pl.BlockSpec((1, 1, bn), lambda l, j: (l, 0, j))],
        out_specs=pl.BlockSpec((1, 8, bn), lambda l, j: (l, 0, j)),
        out_shape=jax.ShapeDtypeStruct((depth, 8, n_mod), F32),
        compiler_params=pltpu.CompilerParams(dimension_semantics=("arbitrary", "arbitrary"),
                                             vmem_limit_bytes=VMEM_LIMIT),
        name="modulation",
    )(cc, w_mod, b_mod.reshape(depth, 1, n_mod))


def _inproj_kernel(x_ref, g_ref, sh_ref, sc_ref, w_ref, cw_ref, cb_ref,
                   z_ref, xs_ref, bc_ref, gqk_ref, gv_ref, gg_ref, hq_ref, hf_ref, hi_ref, hgate_ref,
                   small_ref, *, row_len):
    x = x_ref[...]
    tm = x.shape[0]
    h = _rms(x, g_ref[...]) * (1.0 + sc_ref[...]) + sh_ref[...]
    hb = h.astype(BF16)

    def proj(a, b):
        return jnp.dot(hb, w_ref[:, a:b], preferred_element_type=F32)

    z_ref[...] = proj(_P_Z, _P_XBC)

    pos = lax.broadcasted_iota(jnp.int32, (tm, 128), 0) & (row_len - 1)
    shifts = [k - SSD_CONV // 2 for k in range(SSD_CONV)]
    valid = {s: ((pos + s >= 0) & (pos + s < row_len)) for s in shifts if s != 0}

    def conv_block(uj, lo):
        acc = uj * cw_ref[SSD_CONV // 2:SSD_CONV // 2 + 1, lo:lo + 128] + cb_ref[:, lo:lo + 128]
        for k, s in enumerate(shifts):
            if s == 0:
                continue
            us = pltpu.roll(uj, (-s) % tm, axis=0)
            acc = acc + jnp.where(valid[s], us, 0.0) * cw_ref[k:k + 1, lo:lo + 128]
        v = _silu(acc)
        if lo < SSD_INNER:
            xs_ref[:, lo:lo + 128] = v
        else:
            bc_ref[:, lo - SSD_INNER:lo - SSD_INNER + 128] = v.astype(BF16)

    def p_gqk():
        gqk_ref[:, 0:MIXW] = proj(_P_GQ, _P_GK) * (GLA_HEAD_K ** -0.5)
        gqk_ref[:, MIXW:2 * MIXW] = proj(_P_GK, _P_GV)

    def p_gv_gg():
        gv_ref[...] = proj(_P_GV, _P_GG).astype(BF16)
        gg_ref[...] = proj(_P_GG, _P_HQ)

    def p_hq():
        hq_ref[...] = _silu(proj(_P_HQ, _P_HF))

    def p_hf():
        hf_ref[...] = proj(_P_HF, _P_HI)

    def p_hi_hgate():
        hi_ref[...] = proj(_P_HI, _P_HGATE).astype(BF16)
        hgate_ref[...] = proj(_P_HGATE, _P_SMALL)

    def p_small():
        small_ref[...] = proj(_P_SMALL, _P_END)

    others = [p_gqk, p_gv_gg, p_hq, p_hf, p_hi_hgate, p_small]
    cw = 256
    for j in range(SSD_CONV_DIM // cw):
        u = proj(_P_XBC + j * cw, _P_XBC + (j + 1) * cw)
        others[j]()
        for i in range(cw // 128):
            conv_block(u[:, i * 128:(i + 1) * 128], j * cw + i * 128)


def _inproj(x2, gain, shift, scale, w_p, conv_w, conv_b, row_len):
    t = x2.shape[0]
    tm = min(TM_PROJ, t)
    row = lambda w: pl.BlockSpec((1, w), lambda i: (0, 0))
    tile = lambda w: pl.BlockSpec((tm, w), lambda i: (i, 0))
    outs = [(D, F32), (SSD_INNER, F32), (512, BF16), (2 * MIXW, F32), (MIXW, BF16), (MIXW, F32),
            (MIXW, F32), (2 * MIXW, F32), (MIXW, BF16), (MIXW, F32), (128, F32)]
    return pl.pallas_call(
        functools.partial(_inproj_kernel, row_len=row_len),
        grid=(t // tm,),
        in_specs=[tile(D), row(D), row(D), row(D),
                  pl.BlockSpec((D, _P_END), lambda i: (0, 0), pipeline_mode=pl.Buffered(1)),
                  pl.BlockSpec((SSD_CONV, SSD_CONV_DIM), lambda i: (0, 0)),
                  row(SSD_CONV_DIM)],
        out_specs=[tile(w) for w, _ in outs],
        out_shape=[jax.ShapeDtypeStruct((t, w), dt) for w, dt in outs],
        compiler_params=pltpu.CompilerParams(dimension_semantics=("arbitrary",), vmem_limit_bytes=VMEM_LIMIT),
        name="inproj",
    )(x2, gain, shift, scale, w_p, conv_w, conv_b)


def _pair_block_diag(a):
    z = jnp.zeros((a.shape[0], HEAD_W), a.dtype)
    return jnp.concatenate([jnp.concatenate([a[:, :HEAD_W], z], axis=1),
                            jnp.concatenate([z, a[:, HEAD_W:]], axis=1)], axis=0)


def _gla_chunks(streams, st_ref, cm_ref, mask_ref):
    def blk(x, i):
        return x[i * CHUNK:(i + 1) * CHUNK]

    xs, ks = [], []
    for (_, _, d, prep) in streams:
        k, g = prep()
        ks.append(k)
        g2 = g * LOG2E
        e = jnp.dot(cm_ref[d], jnp.concatenate(_split3(g2), axis=0), preferred_element_type=F32)
        xs.append(jnp.exp2(e))

    atts = []
    for (q, _, d, _), k, x in zip(streams, ks, xs):
        qb, kb = q.astype(BF16), k.astype(BF16)
        for p in range(HEADS // 2):
            sl = slice(p * PAIR_W, (p + 1) * PAIR_W)
            qp, kp = qb[:, sl], kb[:, sl]
            att = mask_ref[d, 0] * _dot_nt(qp, _pair_block_diag(kp))
            for lev in range(1, N_LEVELS + 1):
                xl = blk(x, lev)[:, sl].astype(BF16)
                att = att + mask_ref[d, lev] * _dot_nt(qp * xl, _pair_block_diag(kp * xl))
            atts.append(att)

    outs = []
    zs = jnp.zeros((HEAD_W, HEAD_W), BF16)
    for si, ((q, v, d, _), k, x) in enumerate(zip(streams, ks, xs)):
        edge_row = (CHUNK - 1) if d == 0 else 0
        xb = blk(x, 0)
        xe = blk(x, N_LEVELS + 1)
        parts = []
        for p in range(HEADS // 2):
            sl = slice(p * PAIR_W, (p + 1) * PAIR_W)
            vp = v[:, sl]
            st0, st1 = st_ref[d, 2 * p], st_ref[d, 2 * p + 1]
            st_bd = jnp.concatenate([jnp.concatenate([st0.astype(BF16), zs], axis=1),
                                     jnp.concatenate([zs, st1.astype(BF16)], axis=1)], axis=0)
            o = jnp.dot(atts[si * (HEADS // 2) + p].astype(BF16), _pair_block_diag(vp),
                        preferred_element_type=F32)
            o = o + _dot_nt((q[:, sl] * xb[:, sl]).astype(BF16), st_bd)
            kx = (k[:, sl] * xe[:, sl]).astype(BF16)
            decay = xb[edge_row:edge_row + 1, sl]
            for i, st in enumerate((st0, st1)):
                hs = slice(i * HEAD_W, (i + 1) * HEAD_W)
                st_ref[d, 2 * p + i] = st * decay[:, hs] + _dot_tn(vp[:, hs], kx[:, hs])
            parts.append(o)
        outs.append(jnp.concatenate(parts, axis=1))
    return outs


def _scan_loop(n_chunks, body):
    def step(c, carry):
        body(c)
        return carry
    lax.fori_loop(0, n_chunks, step, 0)


def _gla_scan_kernel(qf_ref, kf_ref, vf_ref, sf_ref, qb_ref, kb_ref, vb_ref, sb_ref,
                     wg_ref, bg_ref, cm_ref, mask_ref, s0_ref,
                     of_ref, ob_ref, st_ref):
    @pl.when(pl.program_id(0) == 0)
    def _():
        st_ref[...] = s0_ref[...]

    n_chunks = qf_ref.shape[0] // CHUNK

    def gate(small, d):
        logit = jnp.dot(small.astype(BF16), wg_ref[d], preferred_element_type=F32) + bg_ref[d]
        return _log_sigmoid(logit) / GLA_NORMALIZER

    def body(cu):
        streams, dests = [], []
        for u in range(SCAN_UNROLL):
            c = cu * SCAN_UNROLL + u
            rf = pl.ds(pl.multiple_of(c * CHUNK, CHUNK), CHUNK)
            rb = pl.ds(pl.multiple_of((n_chunks - 1 - c) * CHUNK, CHUNK), CHUNK)
            kg_f = (kf_ref[rf, :], gate(sf_ref[rf, :], 0))
            kg_b = (kb_ref[rb, :], gate(sb_ref[rb, :], 1))
            streams += [(qf_ref[rf, :], vf_ref[rf, :], 0, lambda kg=kg_f: kg),
                        (qb_ref[rb, :], vb_ref[rb, :], 1, lambda kg=kg_b: kg)]
            dests += [(of_ref, rf), (ob_ref, rb)]
        outs = _gla_chunks(streams, st_ref, cm_ref, mask_ref)
        for (ref, rows), o in zip(dests, outs):
            ref[rows, :] = o

    _scan_loop(n_chunks // SCAN_UNROLL, body)


def _hg_scan_kernel(qf_ref, ff_ref, vf_ref, qb_ref, fb_ref, vb_ref,
                    lb_ref, cm_ref, mask_ref, s0_ref,
                    of_ref, ob_ref, st_ref):
    @pl.when(pl.program_id(0) == 0)
    def _():
        st_ref[...] = s0_ref[...]

    n_chunks = qf_ref.shape[0] // CHUNK
    lb = lb_ref[...]
    log_lb = jnp.log(lb)
    log1m_lb = jnp.log1p(-lb)
    one_m_lb = 1.0 - lb

    def gates(hf):
        e = jnp.exp(-jnp.abs(hf))
        log_sig = jnp.minimum(hf, 0.0) - jnp.log1p(e)
        t = log1m_lb + log_sig
        m = jnp.maximum(log_lb, t)
        log_f = m + jnp.log1p(jnp.exp(-jnp.abs(log_lb - t)))
        k = one_m_lb * (jnp.where(hf >= 0.0, e, 1.0) / (1.0 + e))
        return k, log_f

    def body(cu):
        streams, dests = [], []
        for u in range(SCAN_UNROLL):
            c = cu * SCAN_UNROLL + u
            rf = pl.ds(pl.multiple_of(c * CHUNK, CHUNK), CHUNK)
            rb = pl.ds(pl.multiple_of((n_chunks - 1 - c) * CHUNK, CHUNK), CHUNK)
            streams += [(qf_ref[rf, :], vf_ref[rf, :], 0, lambda rows=rf: gates(ff_ref[rows, :])),
                        (qb_ref[rb, :], vb_ref[rb, :], 1, lambda rows=rb: gates(fb_ref[rows, :]))]
            dests += [(of_ref, rf), (ob_ref, rb)]
        outs = _gla_chunks(streams, st_ref, cm_ref, mask_ref)
        for (ref, rows), o in zip(dests, outs):
            ref[rows, :] = o

    _scan_loop(n_chunks // SCAN_UNROLL, body)


def _const_spec(shape):
    nd = len(shape)
    return pl.BlockSpec(shape, lambda i: (0,) * nd)


def _scan_specs(t):
    tb = min(TB_SCAN, t)
    nb = t // tb
    fwd = lambda w, j: pl.BlockSpec((tb, w), lambda i: (i, j))
    bwd = lambda w, j: pl.BlockSpec((tb, w), lambda i: (nb - 1 - i, j))
    return tb, nb, fwd, bwd


def _gla_scan(gqk, gv, small, wg, bg, cm, mask, s0):
    t = gqk.shape[0]
    tb, nb, fwd, bwd = _scan_specs(t)
    st_shape = (2, HEADS, HEAD_W, HEAD_W)
    return pl.pallas_call(
        _gla_scan_kernel,
        grid=(nb,),
        in_specs=[fwd(MIXW, 0), fwd(MIXW, 1), fwd(MIXW, 0), fwd(128, 0),
                  bwd(MIXW, 0), bwd(MIXW, 1), bwd(MIXW, 0), bwd(128, 0),
                  _const_spec(wg.shape), _const_spec(bg.shape), _const_spec(cm.shape), _const_spec(mask.shape),
                  _const_spec(st_shape)],
        out_specs=[fwd(MIXW, 0), bwd(MIXW, 0), _const_spec(st_shape)],
        out_shape=[jax.ShapeDtypeStruct((t, MIXW), F32), jax.ShapeDtypeStruct((t, MIXW), F32),
                   jax.ShapeDtypeStruct(st_shape, F32)],
        compiler_params=pltpu.CompilerParams(dimension_semantics=("arbitrary",), vmem_limit_bytes=VMEM_LIMIT),
        name="gla_scan",
    )(gqk, gqk, gv, small, gqk, gqk, gv, small, wg, bg, cm, mask, s0)


def _hg_scan(hq, hf, hi, lb, cm, mask, s0):
    t = hq.shape[0]
    tb, nb, fwd, bwd = _scan_specs(t)
    st_shape = (2, HEADS, HEAD_W, HEAD_W)
    return pl.pallas_call(
        _hg_scan_kernel,
        grid=(nb,),
        in_specs=[fwd(MIXW, 0), fwd(MIXW, 0), fwd(MIXW, 0),
                  bwd(MIXW, 0), bwd(MIXW, 1), bwd(MIXW, 0),
                  _const_spec(lb.shape), _const_spec(cm.shape), _const_spec(mask.shape), _const_spec(st_shape)],
        out_specs=[fwd(MIXW, 0), bwd(MIXW, 0), _const_spec(st_shape)],
        out_shape=[jax.ShapeDtypeStruct((t, MIXW), F32), jax.ShapeDtypeStruct((t, MIXW), F32),
                   jax.ShapeDtypeStruct(st_shape, F32)],
        compiler_params=pltpu.CompilerParams(dimension_semantics=("arbitrary",), vmem_limit_bytes=VMEM_LIMIT),
        name="hg_scan",
    )(hq, hf, hi, hq, hf, hi, lb, cm, mask, s0)


def _ssd_chunks(streams, st_ref, dtb_ref, alog_ref, part_ref, cm_ref, ex_ref, diag_ref, causal_ref):
    gw = SSD_INNER // SSD_GROUPS
    lane = lax.broadcasted_iota(jnp.int32, (CHUNK, 128), 1)
    first = lane < SSD_HEADDIM
    zx = jnp.zeros((128, 128), BF16)

    stage = []
    for (xs, _, small, d) in streams:
        dt = small + dtb_ref[d]
        dt = jnp.maximum(dt, 0.0) + jnp.log1p(jnp.exp(-jnp.abs(dt)))
        a = dt * (-jnp.exp(alog_ref[d]))
        ce = jnp.dot(cm_ref[d], jnp.concatenate(_split3(a), axis=0), preferred_element_type=F32)
        acs, aed = ce[0:CHUNK], ce[CHUNK:2 * CHUNK]
        sm = jnp.concatenate([dt, acs, jnp.exp(acs), jnp.exp(aed)], axis=0)
        r1 = sm - sm.astype(BF16).astype(F32)
        r2 = r1 - r1.astype(BF16).astype(F32)
        part = part_ref[d]
        parts = jnp.where(part == 0, sm, jnp.where(part == 1, r1, r2)).astype(BF16)
        big = jnp.dot(parts, ex_ref[d], preferred_element_type=F32)
        dt_x, acs_x = big[0:CHUNK], big[CHUNK:2 * CHUNK]
        dec_x, edge_x = big[2 * CHUNK:3 * CHUNK], big[3 * CHUNK:4 * CHUNK]
        acs_row = jnp.sum(acs_x * diag_ref[...], axis=0, keepdims=True)
        lmat = jnp.exp(jnp.where(causal_ref[d] > 0.0, acs_x - acs_row, -jnp.inf))
        xdt = xs * dt_x
        xed = (xdt * edge_x).astype(BF16)
        stage.append((lmat, xdt, xed, dec_x))

    outs = []
    for (_, bc, _, d), (lmat, xdt, xed, dec_x) in zip(streams, stage):
        edge_row = (CHUNK - 1) if d == 0 else 0
        parts = []
        for grp in range(SSD_GROUPS):
            gl = slice(grp * gw, (grp + 1) * gw)
            b_g = bc[:, grp * SSD_STATE:(grp + 1) * SSD_STATE]
            c_g = bc[:, (SSD_GROUPS + grp) * SSD_STATE:(SSD_GROUPS + grp + 1) * SSD_STATE]
            scores2 = _dot_nt(c_g, jnp.concatenate([b_g, b_g], axis=0))
            scores4 = jnp.concatenate([scores2, scores2], axis=1)
            st = st_ref[d, :, gl]
            y_off = jnp.dot(c_g, st.astype(BF16), preferred_element_type=F32) * dec_x[:, gl]
            for j in range(gw // 256):
                ll = slice(grp * gw + j * 256, grp * gw + (j + 1) * 256)
                w = (scores4 * lmat[:, ll]).astype(BF16)
                xj = xdt[:, ll]
                xm = []
                for i in range(2):
                    xi = xj[:, i * 128:(i + 1) * 128]
                    xm.append(jnp.concatenate([jnp.where(first, xi, 0.0), jnp.where(first, 0.0, xi)],
                                              axis=0).astype(BF16))
                xbd = jnp.concatenate([jnp.concatenate([xm[0], zx], axis=1),
                                       jnp.concatenate([zx, xm[1]], axis=1)], axis=0)
                parts.append(jnp.dot(w, xbd, preferred_element_type=F32) + y_off[:, j * 256:(j + 1) * 256])
            st_ref[d, :, gl] = st * dec_x[edge_row:edge_row + 1, gl] + _dot_tn(b_g, xed[:, gl])
        outs.append(jnp.concatenate(parts, axis=1))
    return outs


def _ssd_scan_kernel(xf_ref, bcf_ref, sf_ref, xb_ref, bcb_ref, sb_ref,
                     dtb_ref, alog_ref, part_ref, cm_ref, ex_ref, diag_ref, causal_ref, s0_ref,
                     of_ref, ob_ref, st_ref):
    @pl.when(pl.program_id(0) == 0)
    def _():
        st_ref[...] = s0_ref[...]

    n_chunks = xf_ref.shape[0] // CHUNK

    def body(cu):
        streams, dests = [], []
        for u in range(SCAN_UNROLL):
            c = cu * SCAN_UNROLL + u
            rf = pl.ds(pl.multiple_of(c * CHUNK, CHUNK), CHUNK)
            rb = pl.ds(pl.multiple_of((n_chunks - 1 - c) * CHUNK, CHUNK), CHUNK)
            streams += [(xf_ref[rf, :], bcf_ref[rf, :], sf_ref[rf, :], 0),
                        (xb_ref[rb, :], bcb_ref[rb, :], sb_ref[rb, :], 1)]
            dests += [(of_ref, rf), (ob_ref, rb)]
        outs = _ssd_chunks(streams, st_ref, dtb_ref, alog_ref, part_ref, cm_ref, ex_ref, diag_ref, causal_ref)
        for (ref, rows), o in zip(dests, outs):
            ref[rows, :] = o

    _scan_loop(n_chunks // SCAN_UNROLL, body)


def _ssd_scan(xs, bc, small, dt_bias, a_log, part, cm, expand, diag, causal, s0):
    t = xs.shape[0]
    tb, nb, fwd, bwd = _scan_specs(t)
    st_shape = (2, SSD_STATE, SSD_INNER)
    return pl.pallas_call(
        _ssd_scan_kernel,
        grid=(nb,),
        in_specs=[fwd(SSD_INNER, 0), fwd(512, 0), fwd(128, 0),
                  bwd(SSD_INNER, 0), bwd(512, 0), bwd(128, 0),
                  _const_spec(dt_bias.shape), _const_spec(a_log.shape), _const_spec(part.shape),
                  _const_spec(cm.shape),
                  _const_spec(expand.shape), _const_spec(diag.shape), _const_spec(causal.shape),
                  _const_spec(st_shape)],
        out_specs=[fwd(SSD_INNER, 0), bwd(SSD_INNER, 0), _const_spec(st_shape)],
        out_shape=[jax.ShapeDtypeStruct((t, SSD_INNER), F32), jax.ShapeDtypeStruct((t, SSD_INNER), F32),
                   jax.ShapeDtypeStruct(st_shape, F32)],
        compiler_params=pltpu.CompilerParams(dimension_semantics=("arbitrary",), vmem_limit_bytes=VMEM_LIMIT),
        name="ssd_scan",
    )(xs, bc, small, xs, bc, small, dt_bias, a_log, part, cm, expand, diag, causal, s0)


def _out_kernel(x_ref, ysf_ref, ysb_ref, xs_ref, z_ref, ygf_ref, ygb_ref, gg_ref, yhf_ref, yhb_ref, hgate_ref,
                dskip_ref, sng_ref, gng_ref, hng_ref, wo_ref, g1_ref, n2g_ref, sh2_ref, sc2_ref, g2_ref,
                w1_ref, w2_ref, fng_ref, o_ref, *, final_norm):
    y_s = ysf_ref[...] + ysb_ref[...] + dskip_ref[...] * xs_ref[...]
    y_s = _rms(y_s * _silu(z_ref[...]), sng_ref[...]).astype(BF16)
    acc = jnp.dot(y_s, wo_ref[0:SSD_INNER, :], preferred_element_type=F32)
    y_g = ygf_ref[...] + ygb_ref[...]
    y_h = yhf_ref[...] + yhb_ref[...]
    gate_g = _silu(gg_ref[...])
    gate_h = _sigmoid(hgate_ref[...])
    for h in range(HEADS):
        sl = slice(h * HEAD_W, (h + 1) * HEAD_W)
        yg = (_rms(y_g[:, sl], gng_ref[...]) * gate_g[:, sl]).astype(BF16)
        yh = (_rms(y_h[:, sl], hng_ref[...]) * gate_h[:, sl]).astype(BF16)
        acc = acc + jnp.dot(yg, wo_ref[SSD_INNER + h * HEAD_W:SSD_INNER + (h + 1) * HEAD_W, :],
                            preferred_element_type=F32)
        acc = acc + jnp.dot(yh, wo_ref[SSD_INNER + MIXW + h * HEAD_W:SSD_INNER + MIXW + (h + 1) * HEAD_W, :],
                            preferred_element_type=F32)
    x1 = x_ref[...] + g1_ref[...] * acc
    hn = (_rms(x1, n2g_ref[...]) * (1.0 + sc2_ref[...]) + sh2_ref[...]).astype(BF16)
    m = jnp.zeros_like(x1)
    fb = 1024
    for j in range(FF // fb):
        a = jnp.maximum(jnp.dot(hn, w1_ref[:, j * fb:(j + 1) * fb], preferred_element_type=F32), 0.0)
        m = m + jnp.dot((a * a).astype(BF16), w2_ref[j * fb:(j + 1) * fb, :], preferred_element_type=F32)
    x2 = x1 + g2_ref[...] * m
    if final_norm:
        x2 = _rms(x2, fng_ref[...])
    o_ref[...] = x2


def _out_block(x2, ysf, ysb, xs, z, ygf, ygb, gg, yhf, yhb, hgate,
               dskip, sng, gng, hng, wo, g1, n2g, sh2, sc2, g2, w1, w2, fng, final_norm):
    t = x2.shape[0]
    tm = min(TM_PROJ, t)
    tile = lambda w: pl.BlockSpec((tm, w), lambda i: (i, 0))
    row = lambda w: pl.BlockSpec((1, w), lambda i: (0, 0))
    resident = lambda shape: pl.BlockSpec(shape, lambda i: (0, 0), pipeline_mode=pl.Buffered(1))
    return pl.pallas_call(
        functools.partial(_out_kernel, final_norm=final_norm),
        grid=(t // tm,),
        in_specs=[tile(D), tile(D), tile(D), tile(D), tile(D),
                  tile(MIXW), tile(MIXW), tile(MIXW), tile(MIXW), tile(MIXW), tile(MIXW),
                  row(D), row(D), row(HEAD_W), row(HEAD_W), resident(wo.shape),
                  row(D), row(D), row(D), row(D), row(D),
                  resident(w1.shape), resident(w2.shape), row(D)],
        out_specs=tile(D),
        out_shape=jax.ShapeDtypeStruct((t, D), F32),
        compiler_params=pltpu.CompilerParams(dimension_semantics=("arbitrary",), vmem_limit_bytes=VMEM_LIMIT),
        name="out_block",
    )(x2, ysf, ysb, xs, z, ygf, ygb, gg, yhf, yhb, hgate,
      dskip, sng, gng, hng, wo, g1, n2g, sh2, sc2, g2, w1, w2, fng)


def _pad_heads(w, heads, width):
    lead = w.shape[:-1]
    w = w.reshape(lead + (heads, width))
    w = jnp.pad(w, [(0, 0)] * len(lead) + [(0, 0), (0, HEAD_W - width)])
    return w.reshape(lead + (heads * HEAD_W,))


def _permute_w_in(w):
    o = _IN_OFFS
    col = lambda i: w[:, o[i]:o[i + 1]]
    dt_f, dt_b = col(2)[:, :SSD_HEADS], col(2)[:, SSD_HEADS:]
    parts = [col(0), col(1),
             _pad_heads(col(3), GLA_HEADS, GLA_HEAD_K), _pad_heads(col(4), GLA_HEADS, GLA_HEAD_K),
             col(5), col(6), col(8), col(9), col(10), col(11)]
    parts += [dt_f] * DT_REP + [dt_b] * DT_REP + [col(7)]
    return jnp.concatenate(parts, axis=1).astype(BF16)


def _dir_lanes(p):
    out = jnp.zeros((2, 1, 128), F32)
    for d in range(2):
        lo = DT_REP * SSD_HEADS * d
        out = out.at[d, 0, lo:lo + DT_REP * SSD_HEADS].set(jnp.tile(p[d], DT_REP))
    return out


def kernel(x, c, ctx, c_ctx, norm1_g, norm2_g, w_mod, b_mod, w_in, ssd_conv_w, ssd_conv_b, ssd_dt_bias,
           ssd_a_log, ssd_d, ssd_norm_g, gla_w_gk2, gla_b_gk, gla_norm_g, hg_lb_logits, hg_norm_g, w_out,
           w_mlp1, w_mlp2, final_norm_g):
    depth = w_in.shape[0]
    assert x.shape[0] == 1 and c.shape[0] == 1 and ctx.shape[0] == 1
    seq, ctx_len = x.shape[1], ctx.shape[1]
    assert seq % TB_SCAN == 0 and ctx_len % CHUNK == 0 and ctx_len <= TB_SCAN

    cm = jnp.asarray(_CM_NP, BF16)
    mask = jnp.asarray(_MASK_NP, F32)
    ssd_cm = jnp.asarray(_SSD_CM_NP, BF16)
    expand = jnp.asarray(_EXPAND_NP, BF16)
    part = jnp.asarray(_PART_NP, jnp.int32)
    diag = jnp.asarray(_DIAG_NP, F32)
    causal = jnp.asarray(_CAUSAL_NP, F32)

    lbs = jnp.cumsum(jax.nn.softmax(hg_lb_logits.astype(F32), axis=0), axis=0)
    lbs = lbs - lbs[0]

    cc = jnp.zeros((8, D), F32).at[0].set(c[0]).at[1].set(c_ctx)
    mod = _modulation(cc, w_mod, b_mod)

    row = lambda v: v.reshape(1, -1)
    xl = x[0]
    xc = ctx[0]
    zeros_g = jnp.zeros((2, HEADS, HEAD_W, HEAD_W), F32)
    zeros_s = jnp.zeros((2, SSD_STATE, SSD_INNER), F32)
    for layer in range(depth):
        last = layer == depth - 1
        w_p = _permute_w_in(w_in[layer])
        wo = w_out[layer].astype(BF16)
        w1 = w_mlp1[layer].astype(BF16)
        w2 = w_mlp2[layer].astype(BF16)
        wg16 = _pad_heads(gla_w_gk2[layer], GLA_HEADS, GLA_HEAD_K).astype(BF16)
        wg = jnp.zeros((2, 128, MIXW), BF16)
        for d in range(2):
            wg = wg.at[d, GLR_LANE + d * GLA_RANK:GLR_LANE + (d + 1) * GLA_RANK].set(wg16[d])
        bg = _pad_heads(gla_b_gk[layer], GLA_HEADS, GLA_HEAD_K).reshape(2, 1, MIXW)
        dtb = _dir_lanes(ssd_dt_bias[layer])
        alog = _dir_lanes(ssd_a_log[layer])
        lb = row(lbs[layer])
        dskip = row(jnp.repeat(ssd_d[layer], SSD_HEADDIM))
        gng, hng = row(gla_norm_g[layer]), row(hg_norm_g[layer])
        sng, n1g, n2g = row(ssd_norm_g[layer]), row(norm1_g[layer]), row(norm2_g[layer])
        fng = row(final_norm_g)
        cw, cb = ssd_conv_w[layer], row(ssd_conv_b[layer])

        def mods(r):
            return [mod[layer, r:r + 1, i * D:(i + 1) * D] for i in range(6)]

        def mix(xin, m, row_len, s_ssd, s_gla, s_hg):
            sh1, sc1 = m[0], m[1]
            z, xs, bc, gqk, gv, gg, hq, hf, hi, hgate, small = _inproj(xin, n1g, sh1, sc1, w_p, cw, cb, row_len)
            ysf, ysb, st_s = _ssd_scan(xs, bc, small, dtb, alog, part, ssd_cm, expand, diag, causal, s_ssd)
            ygf, ygb, st_g = _gla_scan(gqk, gv, small, wg, bg, cm, mask, s_gla)
            yhf, yhb, st_h = _hg_scan(hq, hf, hi, lb, cm, mask, s_hg)
            return (ysf, ysb, xs, z, ygf, ygb, gg, yhf, yhb, hgate), (st_s, st_g, st_h)

        def finish(xin, ys, m, final_norm):
            return _out_block(xin, *ys, dskip, sng, gng, hng, wo, m[2], n2g, m[3], m[4], m[5], w1, w2, fng,
                              final_norm)

        m_ctx, m_lat = mods(1), mods(0)
        ys_ctx, states = mix(xc, m_ctx, ctx_len, zeros_s, zeros_g, zeros_g)
        ys_lat, _ = mix(xl, m_lat, CHUNK, *states)
        xl = finish(xl, ys_lat, m_lat, last)
        if not last:
            xc = finish(xc, ys_ctx, m_ctx, False)
    return xl[None]
```

```python
import functools

import numpy as np
import jax
import jax.numpy as jnp
from jax import lax
from jax.experimental import pallas as pl
from jax.experimental.pallas import tpu as pltpu

F32 = jnp.float32
BF16 = jnp.bfloat16

D = 1024
CHUNK = 64
EPS = 1e-6
SSD_INNER = 1024
SSD_HEADS = 16
SSD_HEADDIM = 64
SSD_STATE = 128
SSD_GROUPS = 2
SSD_CONV = 5
SSD_CONV_DIM = 1536
GLA_HEADS = 4
GLA_HEAD_K = 64
GLA_RANK = 16
GLA_NORMALIZER = 16.0
HEADS = 4
HEAD_W = 128
MIXW = HEADS * HEAD_W
PAIR_W = 2 * HEAD_W
DT_REP = 3
GLR_LANE = 2 * DT_REP * SSD_HEADS
LOG2E = 1.4426950408889634
FF = 4096
N_LEVELS = 6

VMEM_LIMIT = 56 * 1024 * 1024
TM_PROJ = 256
TM_IN = 256
TB_SCAN = 512
SCAN_UNROLL = 4
HG_UNROLL = 2

_IN_SIZES = (1024, 1536, 32, 256, 256, 512, 512, 32, 512, 1024, 512, 512)
_IN_OFFS = np.concatenate([[0], np.cumsum(_IN_SIZES)]).tolist()

_P_Z, _P_XBC, _P_GQ, _P_GK, _P_GV, _P_GG, _P_HQ, _P_HF, _P_HI, _P_HGATE, _P_SMALL, _P_END = (
    0, 1024, 2560, 3072, 3584, 4096, 4608, 5120, 6144, 6656, 7168, 7296)


def _chunk_constants():
    n = CHUNK
    p = np.arange(n)[:, None]
    r = np.arange(n)[None, :]
    blocks = [(r <= p)]
    masks = [(p == r)]
    for lev in range(1, N_LEVELS + 1):
        size = 1 << lev
        half = size // 2
        start = (p // size) * size
        mid = start + half - 1
        second = (p - start) >= half
        m = np.where(second, (r > mid) & (r <= p), (r > p) & (r <= mid))
        blocks.append(m)
        same = (p // size) == (r // size)
        masks.append(same & second & ((r - (r // size) * size) < half))
    blocks.append(r > p)
    fwd = np.concatenate([b.astype(np.float32) for b in blocks], axis=0)
    fmask = np.stack([m.astype(np.float32) for m in masks], axis=0)
    bwd = np.concatenate([b.astype(np.float32)[::-1, ::-1] for b in blocks], axis=0)
    bmask = fmask[:, ::-1, ::-1]
    cm = np.stack([fwd, bwd], axis=0)
    cm = np.concatenate([cm, cm, cm], axis=2)
    mk = np.stack([fmask, bmask], axis=0)
    mk = np.concatenate([mk, mk], axis=3)
    return cm, np.ascontiguousarray(mk)


def _ssd_constants():
    n = CHUNK
    p = np.arange(n)[:, None]
    r = np.arange(n)[None, :]
    cum_f = (r <= p).astype(np.float32)
    edge_f = (r > p).astype(np.float32)
    cm = np.stack([np.concatenate([cum_f, edge_f], 0),
                   np.concatenate([cum_f[::-1, ::-1], edge_f[::-1, ::-1]], 0)], 0)
    cm = np.concatenate([cm, cm, cm], axis=2)
    lanes = np.arange(SSD_INNER)[None, :]
    head_rows = (np.arange(SSD_HEADS)[:, None] == lanes // SSD_HEADDIM).astype(np.float32)
    expand = np.zeros((2, 128, SSD_INNER), np.float32)
    part = np.full((2, 1, 128), 2, np.int32)
    for d in range(2):
        for rep in range(3):
            lo = DT_REP * SSD_HEADS * d + rep * SSD_HEADS
            expand[d, lo:lo + SSD_HEADS] = head_rows
            part[d, 0, lo:lo + SSD_HEADS] = rep
    s_of_lane = lanes % SSD_HEADDIM
    diag = (p == s_of_lane).astype(np.float32)
    causal = np.stack([(p >= s_of_lane), (p <= s_of_lane)], 0).astype(np.float32)
    return cm, expand, part, diag, causal


_CM_NP, _MASK_NP = _chunk_constants()
_SSD_CM_NP, _EXPAND_NP, _PART_NP, _DIAG_NP, _CAUSAL_NP = _ssd_constants()


def _sigmoid(x):
    return 1.0 / (1.0 + jnp.exp(-x))


def _silu(x):
    return x * _sigmoid(x)


def _log_sigmoid(x):
    return jnp.minimum(x, 0.0) - jnp.log1p(jnp.exp(-jnp.abs(x)))


def _split3(x):
    hi = x.astype(BF16)
    r1 = x - hi.astype(F32)
    mid = r1.astype(BF16)
    r2 = r1 - mid.astype(F32)
    lo = r2.astype(BF16)
    return hi, mid, lo


def _exact_left_mul(m01, x):
    hi, mid, lo = _split3(x)
    return (jnp.dot(m01, hi, preferred_element_type=F32)
            + jnp.dot(m01, mid, preferred_element_type=F32)
            + jnp.dot(m01, lo, preferred_element_type=F32))


def _dot_nt(a, b):
    return lax.dot_general(a, b, (((1,), (1,)), ((), ())), preferred_element_type=F32)


def _dot_tn(a, b):
    return lax.dot_general(a, b, (((0,), (0,)), ((), ())), preferred_element_type=F32)


def _rms(x, gain):
    return x * lax.rsqrt(jnp.mean(x * x, axis=-1, keepdims=True) + EPS) * gain


def _mod_kernel(cc_ref, w_ref, b_ref, o_ref):
    s = _silu(cc_ref[...])
    o_ref[0] = jnp.dot(s, w_ref[0], precision=lax.Precision.HIGHEST, preferred_element_type=F32) + b_ref[0]


def _modulation(cc, w_mod, b_mod):
    depth, _, n_mod = w_mod.shape
    bn = 1536
    return pl.pallas_call(
        _mod_kernel,
        grid=(depth, n_mod // bn),
        in_specs=[pl.BlockSpec((8, D), lambda l, j: (0, 0)),
                  pl.BlockSpec((1, D, bn), lambda l, j: (l, 0, j)),
                  pl.BlockSpec((1, 1, bn), lambda l, j: (l, 0, j))],
        out_specs=pl.BlockSpec((1, 8, bn), lambda l, j: (l, 0, j)),
        out_shape=jax.ShapeDtypeStruct((depth, 8, n_mod), F32),
        compiler_params=pltpu.CompilerParams(dimension_semantics=("arbitrary", "arbitrary"),
                                             vmem_limit_bytes=VMEM_LIMIT),
        name="modulation",
    )(cc, w_mod, b_mod.reshape(depth, 1, n_mod))


def _inproj_kernel(x_ref, g_ref, sh_ref, sc_ref, w_ref, cw_ref, cb_ref,
                   z_ref, xs_ref, bc_ref, gqk_ref, gv_ref, gg_ref, hq_ref, hf_ref, hi_ref, hgate_ref,
                   small_ref, *, row_len):
    x = x_ref[...]
    tm = x.shape[0]
    h = _rms(x, g_ref[...]) * (1.0 + sc_ref[...]) + sh_ref[...]
    hb = h.astype(BF16)

    def proj(a, b):
        return jnp.dot(hb, w_ref[:, a:b], preferred_element_type=F32)

    z_ref[...] = proj(_P_Z, _P_XBC)

    pos = lax.broadcasted_iota(jnp.int32, (tm, 128), 0) & (row_len - 1)
    shifts = [k - SSD_CONV // 2 for k in range(SSD_CONV)]
    valid = {s: ((pos + s >= 0) & (pos + s < row_len)) for s in shifts if s != 0}

    def conv_block(uj, lo):
        acc = uj * cw_ref[SSD_CONV // 2:SSD_CONV // 2 + 1, lo:lo + 128] + cb_ref[:, lo:lo + 128]
        for k, s in enumerate(shifts):
            if s == 0:
                continue
            us = pltpu.roll(uj, (-s) % tm, axis=0)
            acc = acc + jnp.where(valid[s], us, 0.0) * cw_ref[k:k + 1, lo:lo + 128]
        v = _silu(acc)
        if lo < SSD_INNER:
            xs_ref[:, lo:lo + 128] = v
        else:
            bc_ref[:, lo - SSD_INNER:lo - SSD_INNER + 128] = v.astype(BF16)

    def p_gqk():
        gqk_ref[:, 0:MIXW] = proj(_P_GQ, _P_GK) * (GLA_HEAD_K ** -0.5)
        gqk_ref[:, MIXW:2 * MIXW] = proj(_P_GK, _P_GV)

    def p_gv_gg():
        gv_ref[...] = proj(_P_GV, _P_GG).astype(BF16)
        gg_ref[...] = proj(_P_GG, _P_HQ)

    def p_hq():
        hq_ref[...] = _silu(proj(_P_HQ, _P_HF))

    def p_hf():
        hf_ref[...] = proj(_P_HF, _P_HI)

    def p_hi_hgate():
        hi_ref[...] = proj(_P_HI, _P_HGATE).astype(BF16)
        hgate_ref[...] = proj(_P_HGATE, _P_SMALL)

    def p_small():
        small_ref[...] = proj(_P_SMALL, _P_END)

    others = [p_gqk, p_gv_gg, p_hq, p_hf, p_hi_hgate, p_small]
    cw = 256
    for j in range(SSD_CONV_DIM // cw):
        u = proj(_P_XBC + j * cw, _P_XBC + (j + 1) * cw)
        others[j]()
        for i in range(cw // 128):
            conv_block(u[:, i * 128:(i + 1) * 128], j * cw + i * 128)


def _inproj(x2, gain, shift, scale, w_p, conv_w, conv_b, row_len):
    t = x2.shape[0]
    tm = min(TM_IN, t)
    row = lambda w: pl.BlockSpec((1, w), lambda i: (0, 0))
    tile = lambda w: pl.BlockSpec((tm, w), lambda i: (i, 0))
    outs = [(D, F32), (SSD_INNER, F32), (512, BF16), (2 * MIXW, F32), (MIXW, BF16), (MIXW, F32),
            (MIXW, F32), (2 * MIXW, F32), (MIXW, BF16), (MIXW, F32), (128, F32)]
    return pl.pallas_call(
        functools.partial(_inproj_kernel, row_len=row_len),
        grid=(t // tm,),
        in_specs=[tile(D), row(D), row(D), row(D),
                  pl.BlockSpec((D, _P_END), lambda i: (0, 0), pipeline_mode=pl.Buffered(1)),
                  pl.BlockSpec((SSD_CONV, SSD_CONV_DIM), lambda i: (0, 0)),
                  row(SSD_CONV_DIM)],
        out_specs=[tile(w) for w, _ in outs],
        out_shape=[jax.ShapeDtypeStruct((t, w), dt) for w, dt in outs],
        compiler_params=pltpu.CompilerParams(dimension_semantics=("arbitrary",), vmem_limit_bytes=VMEM_LIMIT),
        name="inproj",
    )(x2, gain, shift, scale, w_p, conv_w, conv_b)


def _pair_block_diag(a):
    z = jnp.zeros((a.shape[0], HEAD_W), a.dtype)
    return jnp.concatenate([jnp.concatenate([a[:, :HEAD_W], z], axis=1),
                            jnp.concatenate([z, a[:, HEAD_W:]], axis=1)], axis=0)


def _gla_chunks(streams, st_ref, cm_ref, mask_ref):
    def blk(x, i):
        return x[i * CHUNK:(i + 1) * CHUNK]

    xs, ks = [], []
    for (_, _, d, prep) in streams:
        k, g = prep()
        ks.append(k)
        g2 = g * LOG2E
        e = jnp.dot(cm_ref[d], jnp.concatenate(_split3(g2), axis=0), preferred_element_type=F32)
        xs.append((jnp.exp2(blk(e, 0)), jnp.exp2(blk(e, N_LEVELS + 1)),
                   jnp.exp2(e[CHUNK:(N_LEVELS + 1) * CHUNK]).astype(BF16)))

    atts = []
    for (q, _, d, _), k, (_, _, xlev) in zip(streams, ks, xs):
        qb, kb = q.astype(BF16), k.astype(BF16)
        for p in range(HEADS // 2):
            sl = slice(p * PAIR_W, (p + 1) * PAIR_W)
            qp, kp = qb[:, sl], kb[:, sl]
            att = mask_ref[d, 0] * _dot_nt(qp, _pair_block_diag(kp))
            for lev in range(1, N_LEVELS + 1):
                xl = blk(xlev, lev - 1)[:, sl]
                att = att + mask_ref[d, lev] * _dot_nt(qp * xl, _pair_block_diag(kp * xl))
            atts.append(att)

    outs = []
    zs = jnp.zeros((HEAD_W, HEAD_W), BF16)
    for si, ((q, v, d, _), k, (xb, xe, _)) in enumerate(zip(streams, ks, xs)):
        edge_row = (CHUNK - 1) if d == 0 else 0
        parts = []
        for p in range(HEADS // 2):
            sl = slice(p * PAIR_W, (p + 1) * PAIR_W)
            vp = v[:, sl]
            st0, st1 = st_ref[d, 2 * p], st_ref[d, 2 * p + 1]
            st_bd = jnp.concatenate([jnp.concatenate([st0.astype(BF16), zs], axis=1),
                                     jnp.concatenate([zs, st1.astype(BF16)], axis=1)], axis=0)
            o = jnp.dot(atts[si * (HEADS // 2) + p].astype(BF16), _pair_block_diag(vp),
                        preferred_element_type=F32)
            o = o + _dot_nt((q[:, sl] * xb[:, sl]).astype(BF16), st_bd)
            kx = (k[:, sl] * xe[:, sl]).astype(BF16)
            decay = xb[edge_row:edge_row + 1, sl]
            for i, st in enumerate((st0, st1)):
                hs = slice(i * HEAD_W, (i + 1) * HEAD_W)
                st_ref[d, 2 * p + i] = st * decay[:, hs] + _dot_tn(vp[:, hs], kx[:, hs])
            parts.append(o)
        outs.append(jnp.concatenate(parts, axis=1))
    return outs


def _scan_loop(n_chunks, body):
    def step(c, carry):
        body(c)
        return carry
    lax.fori_loop(0, n_chunks, step, 0)


def _gla_scan_kernel(qf_ref, kf_ref, vf_ref, sf_ref, qb_ref, kb_ref, vb_ref, sb_ref,
                     wg_ref, bg_ref, cm_ref, mask_ref, s0_ref,
                     of_ref, ob_ref, st_ref):
    @pl.when(pl.program_id(0) == 0)
    def _():
        st_ref[...] = s0_ref[...]

    n_chunks = qf_ref.shape[0] // CHUNK

    def gate(small, d):
        logit = jnp.dot(small.astype(BF16), wg_ref[d], preferred_element_type=F32) + bg_ref[d]
        return _log_sigmoid(logit) / GLA_NORMALIZER

    def body(cu):
        streams, dests = [], []
        for u in range(SCAN_UNROLL):
            c = cu * SCAN_UNROLL + u
            rf = pl.ds(pl.multiple_of(c * CHUNK, CHUNK), CHUNK)
            rb = pl.ds(pl.multiple_of((n_chunks - 1 - c) * CHUNK, CHUNK), CHUNK)
            kg_f = (kf_ref[rf, :], gate(sf_ref[rf, :], 0))
            kg_b = (kb_ref[rb, :], gate(sb_ref[rb, :], 1))
            streams += [(qf_ref[rf, :], vf_ref[rf, :], 0, lambda kg=kg_f: kg),
                        (qb_ref[rb, :], vb_ref[rb, :], 1, lambda kg=kg_b: kg)]
            dests += [(of_ref, rf), (ob_ref, rb)]
        outs = _gla_chunks(streams, st_ref, cm_ref, mask_ref)
        for (ref, rows), o in zip(dests, outs):
            ref[rows, :] = o

    _scan_loop(n_chunks // SCAN_UNROLL, body)


def _hg_scan_kernel(qf_ref, ff_ref, vf_ref, qb_ref, fb_ref, vb_ref,
                    lb_ref, cm_ref, mask_ref, s0_ref,
                    of_ref, ob_ref, st_ref):
    @pl.when(pl.program_id(0) == 0)
    def _():
        st_ref[...] = s0_ref[...]

    n_chunks = qf_ref.shape[0] // CHUNK
    lb = lb_ref[...]
    log_lb = jnp.log(lb)
    log1m_lb = jnp.log1p(-lb)
    one_m_lb = 1.0 - lb

    def gates(hf):
        e = jnp.exp(-jnp.abs(hf))
        log_sig = jnp.minimum(hf, 0.0) - jnp.log1p(e)
        t = log1m_lb + log_sig
        m = jnp.maximum(log_lb, t)
        log_f = m + jnp.log1p(jnp.exp(-jnp.abs(log_lb - t)))
        k = one_m_lb * (jnp.where(hf >= 0.0, e, 1.0) / (1.0 + e))
        return k, log_f

    def body(cu):
        streams, dests = [], []
        for u in range(HG_UNROLL):
            c = cu * HG_UNROLL + u
            rf = pl.ds(pl.multiple_of(c * CHUNK, CHUNK), CHUNK)
            rb = pl.ds(pl.multiple_of((n_chunks - 1 - c) * CHUNK, CHUNK), CHUNK)
            streams += [(qf_ref[rf, :], vf_ref[rf, :], 0, lambda rows=rf: gates(ff_ref[rows, :])),
                        (qb_ref[rb, :], vb_ref[rb, :], 1, lambda rows=rb: gates(fb_ref[rows, :]))]
            dests += [(of_ref, rf), (ob_ref, rb)]
        outs = _gla_chunks(streams, st_ref, cm_ref, mask_ref)
        for (ref, rows), o in zip(dests, outs):
            ref[rows, :] = o

    _scan_loop(n_chunks // HG_UNROLL, body)


def _const_spec(shape):
    nd = len(shape)
    return pl.BlockSpec(shape, lambda i: (0,) * nd)


def _scan_specs(t):
    tb = min(TB_SCAN, t)
    nb = t // tb
    fwd = lambda w, j: pl.BlockSpec((tb, w), lambda i: (i, j))
    bwd = lambda w, j: pl.BlockSpec((tb, w), lambda i: (nb - 1 - i, j))
    return tb, nb, fwd, bwd


def _gla_scan(gqk, gv, small, wg, bg, cm, mask, s0):
    t = gqk.shape[0]
    tb, nb, fwd, bwd = _scan_specs(t)
    st_shape = (2, HEADS, HEAD_W, HEAD_W)
    return pl.pallas_call(
        _gla_scan_kernel,
        grid=(nb,),
        in_specs=[fwd(MIXW, 0), fwd(MIXW, 1), fwd(MIXW, 0), fwd(128, 0),
                  bwd(MIXW, 0), bwd(MIXW, 1), bwd(MIXW, 0), bwd(128, 0),
                  _const_spec(wg.shape), _const_spec(bg.shape), _const_spec(cm.shape), _const_spec(mask.shape),
                  _const_spec(st_shape)],
        out_specs=[fwd(MIXW, 0), bwd(MIXW, 0), _const_spec(st_shape)],
        out_shape=[jax.ShapeDtypeStruct((t, MIXW), F32), jax.ShapeDtypeStruct((t, MIXW), F32),
                   jax.ShapeDtypeStruct(st_shape, F32)],
        compiler_params=pltpu.CompilerParams(dimension_semantics=("arbitrary",), vmem_limit_bytes=VMEM_LIMIT),
        name="gla_scan",
    )(gqk, gqk, gv, small, gqk, gqk, gv, small, wg, bg, cm, mask, s0)


def _hg_scan(hq, hf, hi, lb, cm, mask, s0):
    t = hq.shape[0]
    tb, nb, fwd, bwd = _scan_specs(t)
    st_shape = (2, HEADS, HEAD_W, HEAD_W)
    return pl.pallas_call(
        _hg_scan_kernel,
        grid=(nb,),
        in_specs=[fwd(MIXW, 0), fwd(MIXW, 0), fwd(MIXW, 0),
                  bwd(MIXW, 0), bwd(MIXW, 1), bwd(MIXW, 0),
                  _const_spec(lb.shape), _const_spec(cm.shape), _const_spec(mask.shape), _const_spec(st_shape)],
        out_specs=[fwd(MIXW, 0), bwd(MIXW, 0), _const_spec(st_shape)],
        out_shape=[jax.ShapeDtypeStruct((t, MIXW), F32), jax.ShapeDtypeStruct((t, MIXW), F32),
                   jax.ShapeDtypeStruct(st_shape, F32)],
        compiler_params=pltpu.CompilerParams(dimension_semantics=("arbitrary",), vmem_limit_bytes=VMEM_LIMIT),
        name="hg_scan",
    )(hq, hf, hi, hq, hf, hi, lb, cm, mask, s0)


def _ssd_chunks(streams, st_ref, dtb_ref, alog_ref, part_ref, cm_ref, ex_ref, diag_ref, causal_ref):
    gw = SSD_INNER // SSD_GROUPS
    lane = lax.broadcasted_iota(jnp.int32, (CHUNK, 128), 1)
    first = lane < SSD_HEADDIM
    zx = jnp.zeros((128, 128), BF16)

    stage = []
    for (xs, _, small, d) in streams:
        dt = small + dtb_ref[d]
        dt = jnp.maximum(dt, 0.0) + jnp.log1p(jnp.exp(-jnp.abs(dt)))
        a = dt * (-jnp.exp(alog_ref[d]))
        ce = jnp.dot(cm_ref[d], jnp.concatenate(_split3(a), axis=0), preferred_element_type=F32)
        acs, aed = ce[0:CHUNK], ce[CHUNK:2 * CHUNK]
        sm = jnp.concatenate([dt, acs, jnp.exp(acs), jnp.exp(aed)], axis=0)
        r1 = sm - sm.astype(BF16).astype(F32)
        r2 = r1 - r1.astype(BF16).astype(F32)
        part = part_ref[d]
        parts = jnp.where(part == 0, sm, jnp.where(part == 1, r1, r2)).astype(BF16)
        big = jnp.dot(parts, ex_ref[d], preferred_element_type=F32)
        dt_x, acs_x = big[0:CHUNK], big[CHUNK:2 * CHUNK]
        dec_x, edge_x = big[2 * CHUNK:3 * CHUNK], big[3 * CHUNK:4 * CHUNK]
        acs_row = jnp.sum(acs_x * diag_ref[...], axis=0, keepdims=True)
        lmat = jnp.exp(jnp.where(causal_ref[d] > 0.0, acs_x - acs_row, -jnp.inf))
        xdt = xs * dt_x
        xed = (xdt * edge_x).astype(BF16)
        stage.append((lmat, xdt, xed, dec_x))

    outs = []
    for (_, bc, _, d), (lmat, xdt, xed, dec_x) in zip(streams, stage):
        edge_row = (CHUNK - 1) if d == 0 else 0
        parts = []
        for grp in range(SSD_GROUPS):
            gl = slice(grp * gw, (grp + 1) * gw)
            b_g = bc[:, grp * SSD_STATE:(grp + 1) * SSD_STATE]
            c_g = bc[:, (SSD_GROUPS + grp) * SSD_STATE:(SSD_GROUPS + grp + 1) * SSD_STATE]
            scores2 = _dot_nt(c_g, jnp.concatenate([b_g, b_g], axis=0))
            scores4 = jnp.concatenate([scores2, scores2], axis=1)
            st = st_ref[d, :, gl]
            y_off = jnp.dot(c_g, st.astype(BF16), preferred_element_type=F32) * dec_x[:, gl]
            for j in range(gw // 256):
                ll = slice(grp * gw + j * 256, grp * gw + (j + 1) * 256)
                w = (scores4 * lmat[:, ll]).astype(BF16)
                xj = xdt[:, ll]
                xm = []
                for i in range(2):
                    xi = xj[:, i * 128:(i + 1) * 128]
                    xm.append(jnp.concatenate([jnp.where(first, xi, 0.0), jnp.where(first, 0.0, xi)],
                                              axis=0).astype(BF16))
                xbd = jnp.concatenate([jnp.concatenate([xm[0], zx], axis=1),
                                       jnp.concatenate([zx, xm[1]], axis=1)], axis=0)
                parts.append(jnp.dot(w, xbd, preferred_element_type=F32) + y_off[:, j * 256:(j + 1) * 256])
            st_ref[d, :, gl] = st * dec_x[edge_row:edge_row + 1, gl] + _dot_tn(b_g, xed[:, gl])
        outs.append(jnp.concatenate(parts, axis=1))
    return outs


def _ssd_scan_kernel(xf_ref, bcf_ref, sf_ref, xb_ref, bcb_ref, sb_ref,
                     dtb_ref, alog_ref, part_ref, cm_ref, ex_ref, diag_ref, causal_ref, s0_ref,
                     of_ref, ob_ref, st_ref):
    @pl.when(pl.program_id(0) == 0)
    def _():
        st_ref[...] = s0_ref[...]

    n_chunks = xf_ref.shape[0] // CHUNK

    def body(cu):
        streams, dests = [], []
        for u in range(SCAN_UNROLL):
            c = cu * SCAN_UNROLL + u
            rf = pl.ds(pl.multiple_of(c * CHUNK, CHUNK), CHUNK)
            rb = pl.ds(pl.multiple_of((n_chunks - 1 - c) * CHUNK, CHUNK), CHUNK)
            streams += [(xf_ref[rf, :], bcf_ref[rf, :], sf_ref[rf, :], 0),
                        (xb_ref[rb, :], bcb_ref[rb, :], sb_ref[rb, :], 1)]
            dests += [(of_ref, rf), (ob_ref, rb)]
        outs = _ssd_chunks(streams, st_ref, dtb_ref, alog_ref, part_ref, cm_ref, ex_ref, diag_ref, causal_ref)
        for (ref, rows), o in zip(dests, outs):
            ref[rows, :] = o

    _scan_loop(n_chunks // SCAN_UNROLL, body)


def _ssd_scan(xs, bc, small, dt_bias, a_log, part, cm, expand, diag, causal, s0):
    t = xs.shape[0]
    tb, nb, fwd, bwd = _scan_specs(t)
    st_shape = (2, SSD_STATE, SSD_INNER)
    return pl.pallas_call(
        _ssd_scan_kernel,
        grid=(nb,),
        in_specs=[fwd(SSD_INNER, 0), fwd(512, 0), fwd(128, 0),
                  bwd(SSD_INNER, 0), bwd(512, 0), bwd(128, 0),
                  _const_spec(dt_bias.shape), _const_spec(a_log.shape), _const_spec(part.shape),
                  _const_spec(cm.shape),
                  _const_spec(expand.shape), _const_spec(diag.shape), _const_spec(causal.shape),
                  _const_spec(st_shape)],
        out_specs=[fwd(SSD_INNER, 0), bwd(SSD_INNER, 0), _const_spec(st_shape)],
        out_shape=[jax.ShapeDtypeStruct((t, SSD_INNER), F32), jax.ShapeDtypeStruct((t, SSD_INNER), F32),
                   jax.ShapeDtypeStruct(st_shape, F32)],
        compiler_params=pltpu.CompilerParams(dimension_semantics=("arbitrary",), vmem_limit_bytes=VMEM_LIMIT),
        name="ssd_scan",
    )(xs, bc, small, xs, bc, small, dt_bias, a_log, part, cm, expand, diag, causal, s0)


def _out_kernel(x_ref, ysf_ref, ysb_ref, xs_ref, z_ref, ygf_ref, ygb_ref, gg_ref, yhf_ref, yhb_ref, hgate_ref,
                dskip_ref, sng_ref, gng_ref, hng_ref, wo_ref, g1_ref, n2g_ref, sh2_ref, sc2_ref, g2_ref,
                w1_ref, w2_ref, fng_ref, o_ref, *, final_norm):
    y_s = ysf_ref[...] + ysb_ref[...] + dskip_ref[...] * xs_ref[...]
    y_s = _rms(y_s * _silu(z_ref[...]), sng_ref[...]).astype(BF16)
    acc = jnp.dot(y_s, wo_ref[0:SSD_INNER, :], preferred_element_type=F32)
    y_g = ygf_ref[...] + ygb_ref[...]
    y_h = yhf_ref[...] + yhb_ref[...]
    gate_g = _silu(gg_ref[...])
    gate_h = _sigmoid(hgate_ref[...])
    heads = [slice(h * HEAD_W, (h + 1) * HEAD_W) for h in range(HEADS)]
    yg = jnp.concatenate([(_rms(y_g[:, sl], gng_ref[...]) * gate_g[:, sl]).astype(BF16) for sl in heads], axis=1)
    yh = jnp.concatenate([(_rms(y_h[:, sl], hng_ref[...]) * gate_h[:, sl]).astype(BF16) for sl in heads], axis=1)
    acc = acc + jnp.dot(yg, wo_ref[SSD_INNER:SSD_INNER + MIXW, :], preferred_element_type=F32)
    acc = acc + jnp.dot(yh, wo_ref[SSD_INNER + MIXW:SSD_INNER + 2 * MIXW, :], preferred_element_type=F32)
    x1 = x_ref[...] + g1_ref[...] * acc
    hn = (_rms(x1, n2g_ref[...]) * (1.0 + sc2_ref[...]) + sh2_ref[...]).astype(BF16)
    m = jnp.zeros_like(x1)
    fb = 1024
    for j in range(FF // fb):
        a = jnp.maximum(jnp.dot(hn, w1_ref[:, j * fb:(j + 1) * fb], preferred_element_type=F32), 0.0)
        m = m + jnp.dot((a * a).astype(BF16), w2_ref[j * fb:(j + 1) * fb, :], preferred_element_type=F32)
    x2 = x1 + g2_ref[...] * m
    if final_norm:
        x2 = _rms(x2, fng_ref[...])
    o_ref[...] = x2


def _out_block(x2, ysf, ysb, xs, z, ygf, ygb, gg, yhf, yhb, hgate,
               dskip, sng, gng, hng, wo, g1, n2g, sh2, sc2, g2, w1, w2, fng, final_norm):
    t = x2.shape[0]
    tm = min(TM_PROJ, t)
    tile = lambda w: pl.BlockSpec((tm, w), lambda i: (i, 0))
    row = lambda w: pl.BlockSpec((1, w), lambda i: (0, 0))
    resident = lambda shape: pl.BlockSpec(shape, lambda i: (0, 0), pipeline_mode=pl.Buffered(1))
    return pl.pallas_call(
        functools.partial(_out_kernel, final_norm=final_norm),
        grid=(t // tm,),
        in_specs=[tile(D), tile(D), tile(D), tile(D), tile(D),
                  tile(MIXW), tile(MIXW), tile(MIXW), tile(MIXW), tile(MIXW), tile(MIXW),
                  row(D), row(D), row(HEAD_W), row(HEAD_W), resident(wo.shape),
                  row(D), row(D), row(D), row(D), row(D),
                  resident(w1.shape), resident(w2.shape), row(D)],
        out_specs=tile(D),
        out_shape=jax.ShapeDtypeStruct((t, D), F32),
        compiler_params=pltpu.CompilerParams(dimension_semantics=("arbitrary",), vmem_limit_bytes=VMEM_LIMIT),
        name="out_block",
    )(x2, ysf, ysb, xs, z, ygf, ygb, gg, yhf, yhb, hgate,
      dskip, sng, gng, hng, wo, g1, n2g, sh2, sc2, g2, w1, w2, fng)


def _pad_heads(w, heads, width):
    lead = w.shape[:-1]
    w = w.reshape(lead + (heads, width))
    w = jnp.pad(w, [(0, 0)] * len(lead) + [(0, 0), (0, HEAD_W - width)])
    return w.reshape(lead + (heads * HEAD_W,))


def _permute_w_in(w):
    o = _IN_OFFS
    col = lambda i: w[:, o[i]:o[i + 1]]
    dt_f, dt_b = col(2)[:, :SSD_HEADS], col(2)[:, SSD_HEADS:]
    parts = [col(0), col(1),
             _pad_heads(col(3), GLA_HEADS, GLA_HEAD_K), _pad_heads(col(4), GLA_HEADS, GLA_HEAD_K),
             col(5), col(6), col(8), col(9), col(10), col(11)]
    parts += [dt_f] * DT_REP + [dt_b] * DT_REP + [col(7)]
    return jnp.concatenate(parts, axis=1).astype(BF16)


def _dir_lanes(p):
    out = jnp.zeros((2, 1, 128), F32)
    for d in range(2):
        lo = DT_REP * SSD_HEADS * d
        out = out.at[d, 0, lo:lo + DT_REP * SSD_HEADS].set(jnp.tile(p[d], DT_REP))
    return out


def kernel(x, c, ctx, c_ctx, norm1_g, norm2_g, w_mod, b_mod, w_in, ssd_conv_w, ssd_conv_b, ssd_dt_bias,
           ssd_a_log, ssd_d, ssd_norm_g, gla_w_gk2, gla_b_gk, gla_norm_g, hg_lb_logits, hg_norm_g, w_out,
           w_mlp1, w_mlp2, final_norm_g):
    depth = w_in.shape[0]
    assert x.shape[0] == 1 and c.shape[0] == 1 and ctx.shape[0] == 1
    seq, ctx_len = x.shape[1], ctx.shape[1]
    assert seq % TB_SCAN == 0 and ctx_len % CHUNK == 0 and ctx_len <= TB_SCAN

    cm = jnp.asarray(_CM_NP, BF16)
    mask = jnp.asarray(_MASK_NP, F32)
    ssd_cm = jnp.asarray(_SSD_CM_NP, BF16)
    expand = jnp.asarray(_EXPAND_NP, BF16)
    part = jnp.asarray(_PART_NP, jnp.int32)
    diag = jnp.asarray(_DIAG_NP, F32)
    causal = jnp.asarray(_CAUSAL_NP, F32)

    lbs = jnp.cumsum(jax.nn.softmax(hg_lb_logits.astype(F32), axis=0), axis=0)
    lbs = lbs - lbs[0]

    cc = jnp.zeros((8, D), F32).at[0].set(c[0]).at[1].set(c_ctx)
    mod = _modulation(cc, w_mod, b_mod)

    row = lambda v: v.reshape(1, -1)
    xl = x[0]
    xc = ctx[0]
    zeros_g = jnp.zeros((2, HEADS, HEAD_W, HEAD_W), F32)
    zeros_s = jnp.zeros((2, SSD_STATE, SSD_INNER), F32)
    for layer in range(depth):
        last = layer == depth - 1
        w_p = _permute_w_in(w_in[layer])
        wo = w_out[layer].astype(BF16)
        w1 = w_mlp1[layer].astype(BF16)
        w2 = w_mlp2[layer].astype(BF16)
        wg16 = _pad_heads(gla_w_gk2[layer], GLA_HEADS, GLA_HEAD_K).astype(BF16)
        wg = jnp.zeros((2, 128, MIXW), BF16)
        for d in range(2):
            wg = wg.at[d, GLR_LANE + d * GLA_RANK:GLR_LANE + (d + 1) * GLA_RANK].set(wg16[d])
        bg = _pad_heads(gla_b_gk[layer], GLA_HEADS, GLA_HEAD_K).reshape(2, 1, MIXW)
        dtb = _dir_lanes(ssd_dt_bias[layer])
        alog = _dir_lanes(ssd_a_log[layer])
        lb = row(lbs[layer])
        dskip = row(jnp.repeat(ssd_d[layer], SSD_HEADDIM))
        gng, hng = row(gla_norm_g[layer]), row(hg_norm_g[layer])
        sng, n1g, n2g = row(ssd_norm_g[layer]), row(norm1_g[layer]), row(norm2_g[layer])
        fng = row(final_norm_g)
        cw, cb = ssd_conv_w[layer], row(ssd_conv_b[layer])

        def mods(r):
            return [mod[layer, r:r + 1, i * D:(i + 1) * D] for i in range(6)]

        def mix(xin, m, row_len, s_ssd, s_gla, s_hg):
            sh1, sc1 = m[0], m[1]
            z, xs, bc, gqk, gv, gg, hq, hf, hi, hgate, small = _inproj(xin, n1g, sh1, sc1, w_p, cw, cb, row_len)
            ysf, ysb, st_s = _ssd_scan(xs, bc, small, dtb, alog, part, ssd_cm, expand, diag, causal, s_ssd)
            ygf, ygb, st_g = _gla_scan(gqk, gv, small, wg, bg, cm, mask, s_gla)
            yhf, yhb, st_h = _hg_scan(hq, hf, hi, lb, cm, mask, s_hg)
            return (ysf, ysb, xs, z, ygf, ygb, gg, yhf, yhb, hgate), (st_s, st_g, st_h)

        def finish(xin, ys, m, final_norm):
            return _out_block(xin, *ys, dskip, sng, gng, hng, wo, m[2], n2g, m[3], m[4], m[5], w1, w2, fng,
                              final_norm)

        m_ctx, m_lat = mods(1), mods(0)
        ys_ctx, states = mix(xc, m_ctx, ctx_len, zeros_s, zeros_g, zeros_g)
        ys_lat, _ = mix(xl, m_lat, CHUNK, *states)
        xl = finish(xl, ys_lat, m_lat, last)
        if not last:
            xc = finish(xc, ys_ctx, m_ctx, False)
    return xl[None]
```
